```python
import jax, jax.numpy as jnp
from jax import lax
import numpy as np

D_MODEL = 1024
BATCH = 2
SEQ = 8192
DEPTH = 1
DEC_BATCH = 1
DEC_SEQ = 16384
PAST_LEN = 128

D_CONV = D_MODEL // 2
D_RWKV = D_MODEL - D_CONV
CONV_WIDTH = 3
HEAD_DIM = 64
N_RWKV_HEADS = D_RWKV // HEAD_DIM
DECAY_RANK = 64
AAA_RANK = 64
GATE_RANK = 128
N_DIR = 2
D_RWKV_IN = 3 * D_RWKV + DECAY_RANK + AAA_RANK + GATE_RANK
D_IN = 3 * D_CONV + D_RWKV_IN
D_FF = 4 * D_MODEL
NORM_EPS = 1e-6
GN_EPS = 64e-5

kernel_name = 'hymba_conv_rwkv7_bidir_encoder'


def _rms_norm(x, g):
    xf = x.astype(jnp.float32)
    y = xf * lax.rsqrt(jnp.mean(xf * xf, axis=-1, keepdims=True) + NORM_EPS)
    return (y * g.astype(jnp.float32)).astype(x.dtype)


def _short_conv_mixer(z, conv_w):
    gate_b, gate_c, u = jnp.split(z, 3, axis=-1)
    cu = gate_c * u
    pad = CONV_WIDTH // 2
    conv = lax.conv_general_dilated(
        cu, conv_w[:, None, :].astype(cu.dtype), window_strides=(1,),
        padding=((pad, pad),), dimension_numbers=('NWC', 'WIO', 'NWC'),
        feature_group_count=D_CONV)
    return gate_b * conv


def _wkv7_scan(r, w, k, v, a, b):
    s0 = jnp.zeros(r.shape[1:] + (HEAD_DIM,), jnp.float32)

    def step(S, inp):
        r_t, w_t, k_t, v_t, a_t, b_t = inp
        sa = jnp.einsum('dbhvk,dbhk->dbhv', S, a_t)
        S = S * w_t[..., None, :] + sa[..., :, None] * b_t[..., None, :] + v_t[..., :, None] * k_t[..., None, :]
        return S, jnp.einsum('dbhvk,dbhk->dbhv', S, r_t)

    _, y = lax.scan(step, s0, (r, w, k, v, a, b))
    return y


def _rwkv7_bidir_mixer(z, mu_shift, w0, w_up, a0, a_up, g_up, k_k, k_a, r_k, ln_x_w, ln_x_b):
    bsz, seq, _ = z.shape
    f32 = jnp.float32
    zs = jnp.stack([z, jnp.flip(z, axis=1)])
    prev = jnp.pad(zs[:, :, :-1], ((0, 0), (0, 0), (1, 0), (0, 0)))
    zs = zs + mu_shift[:, None, None, :] * (prev - zs)
    r, k, v, wl, al, gl = jnp.split(
        zs, [D_RWKV, 2 * D_RWKV, 3 * D_RWKV, 3 * D_RWKV + DECAY_RANK,
             3 * D_RWKV + DECAY_RANK + AAA_RANK], axis=-1)
    w_log = -jax.nn.softplus(-(w0[:, None, None, :] + jnp.einsum('dbtr,drc->dbtc', jnp.tanh(wl), w_up))) - 0.5
    decay = jnp.exp(-jnp.exp(w_log.astype(f32)))
    a = jax.nn.sigmoid(a0[:, None, None, :] + jnp.einsum('dbtr,drc->dbtc', al, a_up))
    g = jnp.einsum('dbtr,drc->dbtc', jax.nn.sigmoid(gl), g_up)

    def heads(t):
        return t.reshape(N_DIR, bsz, seq, N_RWKV_HEADS, HEAD_DIM).astype(f32)

    kk = heads(k * k_k[:, None, None, :])
    kk = kk * lax.rsqrt(jnp.maximum(jnp.sum(kk * kk, axis=-1, keepdims=True), 1e-24))
    k = k * (1.0 + (a - 1.0) * k_a[:, None, None, :])
    r_h, k_h, v_h, a_h, w_h = heads(r), heads(k), heads(v), heads(a), heads(decay)

    def tmaj(t):
        return jnp.moveaxis(t, 2, 0)

    y = _wkv7_scan(tmaj(r_h), tmaj(w_h), tmaj(k_h), tmaj(v_h), tmaj(-kk), tmaj(kk * a_h))
    y = jnp.moveaxis(y, 0, 2)
    mean = jnp.mean(y, axis=-1, keepdims=True)
    var = jnp.mean(jnp.square(y - mean), axis=-1, keepdims=True)
    y = (y - mean) * lax.rsqrt(var + GN_EPS)
    hshape = (N_DIR, 1, 1, N_RWKV_HEADS, HEAD_DIM)
    y = y * ln_x_w.reshape(hshape).astype(f32) + ln_x_b.reshape(hshape).astype(f32)
    bonus = jnp.sum(r_h * k_h * r_k[:, None, None, :, :].astype(f32), axis=-1, keepdims=True) * v_h
    out = (y + bonus).reshape(N_DIR, bsz, seq, D_RWKV) * g.astype(f32)
    out = out[0] + jnp.flip(out[1], axis=1)
    return out.astype(z.dtype)


def _encoder_layer(x, c, w_ada, b_ada, g_pre_mix, g_post_mix, w_in, conv_w, mu_shift, w0, w_up,
                   a0, a_up, g_up, k_k, k_a, r_k, ln_x_w, ln_x_b, w_out, g_pre_ffn, g_post_ffn,
                   w_ffn1, w_ffn2):
    mod = jax.nn.silu(c) @ w_ada + b_ada
    sh1, sc1, gt1, sh2, sc2, gt2 = [m[:, None, :] for m in jnp.split(mod, 6, axis=-1)]
    h = _rms_norm(x, g_pre_mix) * (1.0 + sc1) + sh1
    z = h @ w_in
    y_conv = _short_conv_mixer(z[..., :3 * D_CONV], conv_w)
    y_rwkv = _rwkv7_bidir_mixer(z[..., 3 * D_CONV:], mu_shift, w0, w_up, a0, a_up, g_up,
                                k_k, k_a, r_k, ln_x_w, ln_x_b)
    mix = jnp.concatenate([y_conv, y_rwkv], axis=-1) @ w_out
    x = x + gt1 * _rms_norm(mix, g_post_mix)
    h = _rms_norm(x, g_pre_ffn) * (1.0 + sc2) + sh2
    f = jnp.square(jax.nn.relu(h @ w_ffn1)) @ w_ffn2
    return x + gt2 * _rms_norm(f, g_post_ffn)


def setup_inputs(seed: int = 0) -> dict:
    key = jax.random.key(seed)
    ks = jax.random.split(key, 26)
    f32 = jnp.float32
    L = DEPTH

    def nrm(k, shape, scale):
        return jax.random.normal(k, shape, f32) * scale

    return {
        'x_prompt': nrm(ks[0], (BATCH, SEQ, D_MODEL), 1.0),
        'x_sample': nrm(ks[1], (DEC_BATCH, DEC_SEQ, D_MODEL), 1.0),
        'c_prompt': nrm(ks[2], (BATCH, D_MODEL), 1.0),
        'c_sample': nrm(ks[3], (DEC_BATCH, D_MODEL), 1.0),
        'w_ada': nrm(ks[4], (L, D_MODEL, 6 * D_MODEL), 0.5 * D_MODEL ** -0.5),
        'b_ada': nrm(ks[5], (L, 6 * D_MODEL), 0.02),
        'g_pre_mix': 1.0 + nrm(ks[6], (L, D_MODEL), 0.02),
        'g_post_mix': 1.0 + nrm(ks[7], (L, D_MODEL), 0.02),
        'w_in': nrm(ks[8], (L, D_MODEL, D_IN), D_MODEL ** -0.5),
        'conv_w': nrm(ks[9], (L, CONV_WIDTH, D_CONV), CONV_WIDTH ** -0.5),
        'mu_shift': jax.random.uniform(ks[10], (L, N_DIR, D_RWKV_IN), f32),
        'w0': jax.random.uniform(ks[11], (L, N_DIR, D_RWKV), f32, -6.0, 1.0),
        'w_up': nrm(ks[12], (L, N_DIR, DECAY_RANK, D_RWKV), 0.5 * DECAY_RANK ** -0.5),
        'a0': nrm(ks[13], (L, N_DIR, D_RWKV), 0.1),
        'a_up': nrm(ks[14], (L, N_DIR, AAA_RANK, D_RWKV), 0.5 * AAA_RANK ** -0.5),
        'g_up': nrm(ks[15], (L, N_DIR, GATE_RANK, D_RWKV), GATE_RANK ** -0.5),
        'k_k': 0.85 + nrm(ks[16], (L, N_DIR, D_RWKV), 0.05),
        'k_a': 1.0 + nrm(ks[17], (L, N_DIR, D_RWKV), 0.05),
        'r_k': nrm(ks[18], (L, N_DIR, N_RWKV_HEADS, HEAD_DIM), 0.1),
        'ln_x_w': 1.0 + nrm(ks[19], (L, N_DIR, D_RWKV), 0.02),
        'ln_x_b': nrm(ks[20], (L, N_DIR, D_RWKV), 0.02),
        'w_out': nrm(ks[21], (L, D_MODEL, D_MODEL), D_MODEL ** -0.5),
        'g_pre_ffn': 1.0 + nrm(ks[22], (L, D_MODEL), 0.02),
        'g_post_ffn': 1.0 + nrm(ks[23], (L, D_MODEL), 0.02),
        'w_ffn1': nrm(ks[24], (L, D_MODEL, D_FF), D_MODEL ** -0.5),
        'w_ffn2': nrm(ks[25], (L, D_FF, D_MODEL), D_FF ** -0.5),
    }


def reference(x_prompt, x_sample, c_prompt, c_sample, w_ada, b_ada, g_pre_mix, g_post_mix, w_in,
              conv_w, mu_shift, w0, w_up, a0, a_up, g_up, k_k, k_a, r_k, ln_x_w, ln_x_b, w_out,
              g_pre_ffn, g_post_ffn, w_ffn1, w_ffn2):
    y_prompt = x_prompt
    y_sample = x_sample
    for l in range(DEPTH):
        layer_params = (w_ada[l], b_ada[l], g_pre_mix[l], g_post_mix[l], w_in[l], conv_w[l],
                        mu_shift[l], w0[l], w_up[l], a0[l], a_up[l], g_up[l], k_k[l], k_a[l],
                        r_k[l], ln_x_w[l], ln_x_b[l], w_out[l], g_pre_ffn[l], g_post_ffn[l],
                        w_ffn1[l], w_ffn2[l])
        y_prompt = _encoder_layer(y_prompt, c_prompt, *layer_params)
        y_sample = _encoder_layer(y_sample, c_sample, *layer_params)
    return (y_prompt, y_sample)
```

```python
import functools

import jax
import jax.numpy as jnp
from jax import lax
from jax.experimental import pallas as pl
from jax.experimental.pallas import tpu as pltpu

D_MODEL = 1024
D_CONV = 512
D_RWKV = 512
HEAD_DIM = 64
LORA_WA = 128
GATE_RANK = 128
D_RWKV_IN = 3 * D_RWKV + LORA_WA + GATE_RANK
D_CONV_IN = 3 * D_CONV
D_FF = 4096
NORM_EPS = 1e-6
GN_EPS = 64e-5

CHUNK = 64
PAIR = 2 * HEAD_DIM
N_PAIR = D_RWKV // PAIR
HALO = 8

TILE_IN = 256
TILE_SCAN = 256
TILE_OUT = 256
VMEM_LIMIT = 56 * 1024 * 1024

F32 = jnp.float32
BF16 = jnp.bfloat16
HI = lax.Precision.HIGHEST


def _dot(a, b, precision=None):
    return jnp.dot(a, b, precision=precision, preferred_element_type=F32)


def _dot_nt(a, b, precision=None):
    return lax.dot_general(a, b, (((1,), (1,)), ((), ())), precision=precision,
                           preferred_element_type=F32)


def _dot_tn(a, b, precision=None):
    return lax.dot_general(a, b, (((0,), (0,)), ((), ())), precision=precision,
                           preferred_element_type=F32)


def _sigmoid(x):
    return 1.0 / (1.0 + jnp.exp(-x))


def _rms(x, g):
    return x * lax.rsqrt(jnp.mean(x * x, axis=-1, keepdims=True) + NORM_EPS) * g


def _seg_sum(x, ones_bd):
    hi = x.astype(BF16)
    lo = (x - hi.astype(F32)).astype(BF16)
    return _dot(hi, ones_bd) + _dot(lo, ones_bd)


def _mod_kernel(c_ref, w_ref, b_ref, o_ref):
    c = c_ref[...]
    s = c * _sigmoid(c)
    o_ref[...] = _dot(s, w_ref[...], HI) + b_ref[...]


def _modulation(c_all, w_ada, b_ada):
    rows = c_all.shape[0]
    n_blk = w_ada.shape[1] // D_MODEL
    return pl.pallas_call(
        _mod_kernel,
        grid=(n_blk,),
        in_specs=[pl.BlockSpec((rows, D_MODEL), lambda j: (0, 0)),
                  pl.BlockSpec((D_MODEL, D_MODEL), lambda j: (0, j)),
                  pl.BlockSpec((1, D_MODEL), lambda j: (0, j))],
        out_specs=pl.BlockSpec((rows, D_MODEL), lambda j: (0, j)),
        out_shape=jax.ShapeDtypeStruct((rows, w_ada.shape[1]), F32),
        name="adaln_mod",
    )(c_all, w_ada, b_ada)


def _inproj_kernel(x_ref, mod_ref, g_ref, w_ref, zc_ref, zr_ref):
    m = mod_ref[0]
    h = _rms(x_ref[0], g_ref[...]) * (1.0 + m[1:2]) + m[0:1]
    hb = h.astype(BF16)
    zc_ref[0] = _dot(hb, w_ref[:, :D_CONV_IN])
    zr_ref[0] = _dot(hb, w_ref[:, D_CONV_IN:])


def _inproj(x, mod, g_pre, w_in_bf):
    bsz, seq, _ = x.shape
    tm = TILE_IN
    d_in = w_in_bf.shape[1]
    return pl.pallas_call(
        _inproj_kernel,
        grid=(bsz, seq // tm),
        in_specs=[pl.BlockSpec((1, tm, D_MODEL), lambda b, i: (b, i, 0)),
                  pl.BlockSpec((1, 6, D_MODEL), lambda b, i: (b, 0, 0)),
                  pl.BlockSpec((1, D_MODEL), lambda b, i: (0, 0)),
                  pl.BlockSpec((D_MODEL, d_in), lambda b, i: (0, 0))],
        out_specs=[pl.BlockSpec((1, tm, D_CONV_IN), lambda b, i: (b, i, 0)),
                   pl.BlockSpec((1, tm, D_RWKV_IN), lambda b, i: (b, i, 0))],
        out_shape=[jax.ShapeDtypeStruct((bsz, seq, D_CONV_IN), F32),
                   jax.ShapeDtypeStruct((bsz, seq, D_RWKV_IN), F32)],
        compiler_params=pltpu.CompilerParams(
            dimension_semantics=("arbitrary", "arbitrary"), vmem_limit_bytes=VMEM_LIMIT),
        name="in_proj",
    )(x, mod, g_pre, w_in_bf)


def _block_diag(y, lane_lo):
    return jnp.concatenate([jnp.where(lane_lo, y, 0.0), jnp.where(lane_lo, 0.0, y)], axis=0)


def _chunk_step(r, k, v, a, b, le, s_bd, tri, strict, incl, eye, lane_lo, bd_mask):
    cum = _dot(tri, le, HI)
    tot = jnp.sum(le, axis=0, keepdims=True)
    off = 0.5 * tot
    ci = cum - off
    e_in = jnp.exp(ci)
    e_ex = jnp.exp(ci - le)
    e_neg = jnp.exp(-ci)
    e_off = jnp.exp(off)
    e_end = jnp.exp(tot - off)
    p_all = jnp.exp(tot)
    at = a * e_ex
    rt = r * e_in
    bt = b * e_neg
    kt = k * e_neg
    bh = bt * e_end
    kh = kt * e_end

    ys = []
    s_new = []
    for p in range(N_PAIR):
        sl = slice(p * PAIR, (p + 1) * PAIR)
        at_p, rt_p, bt_p, kt_p, v_p = at[:, sl], rt[:, sl], bt[:, sl], kt[:, sl], v[:, sl]
        s_p = s_bd[p]
        s_off = s_p * e_off[:, sl]
        v_bd = _block_diag(v_p, lane_lo)
        aa = _dot_nt(jnp.concatenate([at_p, rt_p], axis=0),
                     jnp.concatenate([_block_diag(bt_p, lane_lo), _block_diag(kt_p, lane_lo)], axis=0),
                     HI)
        n_ab = jnp.where(strict, aa[:CHUNK, :PAIR], 0.0)
        a_ak = jnp.where(strict, aa[:CHUNK, PAIR:], 0.0)
        a_rb = jnp.where(incl, aa[CHUNK:, :PAIR], 0.0)
        a_rk = jnp.where(incl, aa[CHUNK:, PAIR:], 0.0)
        x = eye + n_ab
        pw = _dot(n_ab, _block_diag(n_ab, lane_lo), HI)
        width = 2
        while 2 * width < CHUNK:
            zz = _dot(jnp.concatenate([x, pw], axis=0), _block_diag(pw, lane_lo), HI)
            x = x + zz[:CHUNK]
            pw = zz[CHUNK:]
            width *= 2
        x = x + _dot(x, _block_diag(pw, lane_lo), HI)
        g = _dot_nt(at_p, s_off, HI) + _dot(a_ak, v_bd, HI)
        u = _dot(x, _block_diag(g, lane_lo), HI)
        y = (_dot_nt(rt_p, s_off, HI) + _dot(a_rb, _block_diag(u, lane_lo), HI)
             + _dot(a_rk, v_bd, HI))
        upd = _dot_tn(jnp.concatenate([u, v_p], axis=0),
                      jnp.concatenate([bh[:, sl], kh[:, sl]], axis=0), HI)
        s_new.append(jnp.where(bd_mask, s_p * p_all[:, sl] + upd, 0.0))
        ys.append(y)
    return jnp.concatenate(ys, axis=1), s_new


def _rwkv_kernel(z_ref, halo_ref, mu_ref, w0_ref, wup_ref, a0_ref, aup_ref, gup_ref,
                 kk_ref, ka_ref, rk_ref, lnw_ref, lnb_ref, ones_ref,
                 o_ref,
                 s_ref, r_s, k_s, v_s, a_s, b_s, le_s, g_s, y_s):
    d = pl.program_id(1)
    i = pl.program_id(2)
    tt = z_ref.shape[1]
    n_chunk = tt // CHUNK

    @pl.when(i == 0)
    def _():
        s_ref[...] = jnp.zeros_like(s_ref)

    fwd = d == 0
    z = z_ref[0]
    halo = halo_ref[0]
    edge = jnp.where(fwd, halo[HALO - 1:HALO], halo[0:1])
    edge = jnp.where(i > 0, edge, 0.0)
    rows = lax.broadcasted_iota(jnp.int32, (tt, 1), 0)
    rolled = pltpu.roll(z, jnp.where(fwd, 1, tt - 1), 0)
    prev = jnp.where(rows == jnp.where(fwd, 0, tt - 1), edge, rolled)
    zs = z + mu_ref[0] * (prev - z)

    r = zs[:, 0:D_RWKV]
    k = zs[:, D_RWKV:2 * D_RWKV]
    v = zs[:, 2 * D_RWKV:3 * D_RWKV]
    lora = zs[:, 3 * D_RWKV:3 * D_RWKV + LORA_WA]
    gl = zs[:, 3 * D_RWKV + LORA_WA:]
    ones_bd = ones_ref[...]

    w_pre = w0_ref[0] + _dot(jnp.tanh(lora), wup_ref[0], HI)
    le_s[...] = _sigmoid(w_pre) * (-0.6065306597126334)
    a_sig = _sigmoid(a0_ref[0] + _dot(lora, aup_ref[0], HI))
    g_s[...] = _dot(_sigmoid(gl), gup_ref[0], HI)
    kk = k * kk_ref[0]
    kk = kk * lax.rsqrt(jnp.maximum(_seg_sum(kk * kk, ones_bd), 1e-24))
    r_s[...] = r
    k_s[...] = k * (1.0 + (a_sig - 1.0) * ka_ref[0])
    v_s[...] = v
    a_s[...] = -kk
    b_s[...] = kk * a_sig

    sign = jnp.where(fwd, 1, -1)
    t_row = lax.broadcasted_iota(jnp.int32, (CHUNK, CHUNK), 0)
    t_col = lax.broadcasted_iota(jnp.int32, (CHUNK, CHUNK), 1)
    tri = jnp.where((t_row - t_col) * sign >= 0, 1.0, 0.0).astype(F32)
    p_row = lax.broadcasted_iota(jnp.int32, (CHUNK, PAIR), 0)
    p_lane = lax.broadcasted_iota(jnp.int32, (CHUNK, PAIR), 1)
    p_col = jnp.bitwise_and(p_lane, HEAD_DIM - 1)
    delta = (p_row - p_col) * sign
    strict = delta > 0
    incl = delta >= 0
    eye = jnp.where(delta == 0, 1.0, 0.0).astype(F32)
    lane_lo = p_lane < HEAD_DIM
    q_row = lax.broadcasted_iota(jnp.int32, (PAIR, PAIR), 0)
    q_lane = lax.broadcasted_iota(jnp.int32, (PAIR, PAIR), 1)
    bd_mask = (q_row < HEAD_DIM) == (q_lane < HEAD_DIM)

    def body(j, carry):
        c = jnp.where(fwd, j, n_chunk - 1 - j)
        rs = pl.ds(pl.multiple_of(c * CHUNK, CHUNK), CHUNK)
        s_bd = [s_ref[p] for p in range(N_PAIR)]
        y, s_new = _chunk_step(r_s[rs, :], k_s[rs, :], v_s[rs, :], a_s[rs, :], b_s[rs, :],
                               le_s[rs, :], s_bd, tri, strict, incl, eye, lane_lo, bd_mask)
        for p in range(N_PAIR):
            s_ref[p] = s_new[p]
        y_s[rs, :] = y
        return carry

    lax.fori_loop(0, n_chunk, body, 0)

    y = y_s[...]
    mean = _seg_sum(y, ones_bd) * (1.0 / HEAD_DIM)
    yc = y - mean
    var = _seg_sum(yc * yc, ones_bd) * (1.0 / HEAD_DIM)
    yn = yc * lax.rsqrt(var + GN_EPS) * lnw_ref[0] + lnb_ref[0]
    bonus = _seg_sum(r_s[...] * k_s[...] * rk_ref[0], ones_bd) * v_s[...]
    o_ref[0, 0] = (yn + bonus) * g_s[...]


def _rwkv(z_rwkv, mu, w0, wup_pad, a0, aup_pad, gup, k_k, k_a, r_k, ln_w, ln_b, ones_bd):
    bsz, seq, _ = z_rwkv.shape
    tt = TILE_SCAN
    n_tile = seq // tt
    blk = tt // HALO

    def tile_idx(d, i):
        return i + d * (n_tile - 1 - 2 * i)

    def halo_idx(d, i):
        t = tile_idx(d, i)
        prev_blk = jnp.maximum(t * blk - 1, 0)
        next_blk = jnp.minimum((t + 1) * blk, n_tile * blk - 1)
        return jnp.where(d == 0, prev_blk, next_blk)

    def per_dir(width):
        return pl.BlockSpec((1, 1, width), lambda b, d, i: (d, 0, 0))

    def per_dir_mat(rows, width):
        return pl.BlockSpec((1, rows, width), lambda b, d, i: (d, 0, 0))

    scratch = [pltpu.VMEM((N_PAIR, PAIR, PAIR), F32)]
    scratch += [pltpu.VMEM((tt, D_RWKV), F32) for _ in range(8)]
    return pl.pallas_call(
        _rwkv_kernel,
        grid=(bsz, 2, n_tile),
        in_specs=[pl.BlockSpec((1, tt, D_RWKV_IN), lambda b, d, i: (b, tile_idx(d, i), 0)),
                  pl.BlockSpec((1, HALO, D_RWKV_IN), lambda b, d, i: (b, halo_idx(d, i), 0)),
                  per_dir(D_RWKV_IN), per_dir(D_RWKV), per_dir_mat(LORA_WA, D_RWKV),
                  per_dir(D_RWKV), per_dir_mat(LORA_WA, D_RWKV), per_dir_mat(GATE_RANK, D_RWKV),
                  per_dir(D_RWKV), per_dir(D_RWKV), per_dir(D_RWKV), per_dir(D_RWKV),
                  per_dir(D_RWKV),
                  pl.BlockSpec((D_RWKV, D_RWKV), lambda b, d, i: (0, 0))],
        out_specs=pl.BlockSpec((1, 1, tt, D_RWKV), lambda b, d, i: (d, b, tile_idx(d, i), 0)),
        out_shape=jax.ShapeDtypeStruct((2, bsz, seq, D_RWKV), F32),
        scratch_shapes=scratch,
        compiler_params=pltpu.CompilerParams(
            dimension_semantics=("arbitrary", "arbitrary", "arbitrary"),
            vmem_limit_bytes=VMEM_LIMIT),
        name="rwkv7_scan",
    )(z_rwkv, z_rwkv, mu, w0, wup_pad, a0, aup_pad, gup, k_k, k_a, r_k, ln_w, ln_b, ones_bd)


def _out_kernel(x_ref, zc_ref, zp_ref, zn_ref, o_ref, mod_ref, cw_ref, g_post_mix_ref,
                g_pre_ffn_ref, g_post_ffn_ref, wout_ref, w1_ref, w2_ref, y_ref):
    i = pl.program_id(1)
    n = pl.num_programs(1)
    tm = x_ref.shape[1]
    m = mod_ref[0]
    x = x_ref[0]

    zc = zc_ref[0]
    gate_b = zc[:, 0:D_CONV]
    cu = zc[:, D_CONV:2 * D_CONV] * zc[:, 2 * D_CONV:]
    zp = zp_ref[0]
    zn = zn_ref[0]
    cu_prev = zp[HALO - 1:HALO, D_CONV:2 * D_CONV] * zp[HALO - 1:HALO, 2 * D_CONV:]
    cu_next = zn[0:1, D_CONV:2 * D_CONV] * zn[0:1, 2 * D_CONV:]
    cu_prev = jnp.where(i > 0, cu_prev, 0.0)
    cu_next = jnp.where(i < n - 1, cu_next, 0.0)
    rows = lax.broadcasted_iota(jnp.int32, (tm, 1), 0)
    before = jnp.where(rows == 0, cu_prev, pltpu.roll(cu, 1, 0))
    after = jnp.where(rows == tm - 1, cu_next, pltpu.roll(cu, tm - 1, 0))
    cw = cw_ref[...]
    y_conv = gate_b * (cw[0:1] * before + cw[1:2] * cu + cw[2:3] * after)

    y_rwkv = o_ref[0, 0] + o_ref[1, 0]
    mix = (_dot(y_conv.astype(BF16), wout_ref[:D_CONV, :])
           + _dot(y_rwkv.astype(BF16), wout_ref[D_CONV:, :]))
    x1 = x + m[2:3] * _rms(mix, g_post_mix_ref[...])

    h = _rms(x1, g_pre_ffn_ref[...]) * (1.0 + m[4:5]) + m[3:4]
    f = _dot(h.astype(BF16), w1_ref[...])
    f = jnp.square(jnp.maximum(f, 0.0))
    f = _dot(f.astype(BF16), w2_ref[...])
    y_ref[0] = x1 + m[5:6] * _rms(f, g_post_ffn_ref[...])


def _outproj_ffn(x, z_conv, o_dirs, mod, conv_w_pad, g_post_mix, g_pre_ffn, g_post_ffn,
                 w_out_bf, w1_bf, w2_bf):
    bsz, seq, _ = x.shape
    tm = TILE_OUT
    n_tile = seq // tm
    blk = tm // HALO

    def const(shape):
        return pl.BlockSpec(shape, lambda b, i: (0,) * len(shape),
                            pipeline_mode=pl.Buffered(1))

    return pl.pallas_call(
        _out_kernel,
        grid=(bsz, n_tile),
        in_specs=[pl.BlockSpec((1, tm, D_MODEL), lambda b, i: (b, i, 0)),
                  pl.BlockSpec((1, tm, D_CONV_IN), lambda b, i: (b, i, 0)),
                  pl.BlockSpec((1, HALO, D_CONV_IN),
                               lambda b, i: (b, jnp.maximum(i * blk - 1, 0), 0)),
                  pl.BlockSpec((1, HALO, D_CONV_IN),
                               lambda b, i: (b, jnp.minimum((i + 1) * blk, n_tile * blk - 1), 0)),
                  pl.BlockSpec((2, 1, tm, D_RWKV), lambda b, i: (0, b, i, 0)),
                  pl.BlockSpec((1, 6, D_MODEL), lambda b, i: (b, 0, 0)),
                  const((HALO, D_CONV)), const((1, D_MODEL)), const((1, D_MODEL)),
                  const((1, D_MODEL)), const((D_MODEL, D_MODEL)), const((D_MODEL, D_FF)),
                  const((D_FF, D_MODEL))],
        out_specs=pl.BlockSpec((1, tm, D_MODEL), lambda b, i: (b, i, 0)),
        out_shape=jax.ShapeDtypeStruct((bsz, seq, D_MODEL), F32),
        compiler_params=pltpu.CompilerParams(
            dimension_semantics=("arbitrary", "arbitrary"), vmem_limit_bytes=VMEM_LIMIT),
        name="out_proj_ffn",
    )(x, z_conv, z_conv, z_conv, o_dirs, mod, conv_w_pad, g_post_mix, g_pre_ffn, g_post_ffn,
      w_out_bf, w1_bf, w2_bf)


def _pad_rows(w, top, total):
    return jnp.pad(w, ((0, 0), (top, total - top - w.shape[1]), (0, 0)))


def _layer(xs, cs, w_ada, b_ada, g_pre_mix, g_post_mix, w_in, conv_w, mu_shift, w0, w_up, a0,
           a_up, g_up, k_k, k_a, r_k, ln_x_w, ln_x_b, w_out, g_pre_ffn, g_post_ffn, w_ffn1,
           w_ffn2):
    n_seq = sum(c.shape[0] for c in cs)
    c_all = jnp.pad(jnp.concatenate(cs, axis=0), ((0, HALO - n_seq), (0, 0)))
    mod = _modulation(c_all, w_ada, b_ada[None, :]).reshape(HALO, 6, D_MODEL)

    w_in_bf = w_in.astype(BF16)
    w_out_bf = w_out.astype(BF16)
    w1_bf = w_ffn1.astype(BF16)
    w2_bf = w_ffn2.astype(BF16)
    row = lambda p: p[None, :]
    dir_row = lambda p: p.reshape(2, 1, -1)
    wup_pad = _pad_rows(w_up, 0, LORA_WA)
    aup_pad = _pad_rows(a_up, LORA_WA // 2, LORA_WA)
    conv_w_pad = jnp.pad(conv_w, ((0, HALO - conv_w.shape[0]), (0, 0)))
    head = jnp.arange(D_RWKV) // HEAD_DIM
    ones_bd = (head[:, None] == head[None, :]).astype(BF16)

    outs = []
    start = 0
    for x in xs:
        bsz = x.shape[0]
        m = mod[start:start + bsz]
        start += bsz
        z_conv, z_rwkv = _inproj(x, m, row(g_pre_mix), w_in_bf)
        o_dirs = _rwkv(z_rwkv, dir_row(mu_shift), dir_row(w0), wup_pad, dir_row(a0), aup_pad,
                       g_up, dir_row(k_k), dir_row(k_a), dir_row(r_k), dir_row(ln_x_w),
                       dir_row(ln_x_b), ones_bd)
        outs.append(_outproj_ffn(x, z_conv, o_dirs, m, conv_w_pad, row(g_post_mix),
                                 row(g_pre_ffn), row(g_post_ffn), w_out_bf, w1_bf, w2_bf))
    return outs


def kernel(x_prompt, x_sample, c_prompt, c_sample, w_ada, b_ada, g_pre_mix, g_post_mix, w_in,
           conv_w, mu_shift, w0, w_up, a0, a_up, g_up, k_k, k_a, r_k, ln_x_w, ln_x_b, w_out,
           g_pre_ffn, g_post_ffn, w_ffn1, w_ffn2):
    xs = [x_prompt, x_sample]
    for l in range(w_ada.shape[0]):
        xs = _layer(xs, [c_prompt, c_sample], w_ada[l], b_ada[l], g_pre_mix[l], g_post_mix[l],
                    w_in[l], conv_w[l], mu_shift[l], w0[l], w_up[l], a0[l], a_up[l], g_up[l],
                    k_k[l], k_a[l], r_k[l], ln_x_w[l], ln_x_b[l], w_out[l], g_pre_ffn[l],
                    g_post_ffn[l], w_ffn1[l], w_ffn2[l])
    return (xs[0], xs[1])
```

```python
import numpy as np

import jax
import jax.numpy as jnp
from jax import lax
from jax.experimental import pallas as pl
from jax.experimental.pallas import tpu as pltpu

D_MODEL = 1024
D_CONV = 512
D_RWKV = 512
HEAD_DIM = 64
LORA_WA = 128
GATE_RANK = 128
D_RWKV_IN = 3 * D_RWKV + LORA_WA + GATE_RANK
D_CONV_IN = 3 * D_CONV
D_FF = 4096
NORM_EPS = 1e-6
GN_EPS = 64e-5
DECAY_SCALE = -0.6065306597126334

CHUNK = 64
PAIR = 2 * HEAD_DIM
N_PAIR = D_RWKV // PAIR
HALO = 8

TILE_IN = 256
TILE_SCAN = 256
TILE_OUT = 256
VMEM_LIMIT = 56 * 1024 * 1024

F32 = jnp.float32
BF16 = jnp.bfloat16
HI = lax.Precision.HIGHEST


def _dot(a, b, precision=None):
    return jnp.dot(a, b, precision=precision, preferred_element_type=F32)


def _dot_nt(a, b):
    return lax.dot_general(a, b, (((1,), (1,)), ((), ())), preferred_element_type=F32)


def _dot_tn(a, b):
    return lax.dot_general(a, b, (((0,), (0,)), ((), ())), preferred_element_type=F32)


def _bf(x):
    return x.astype(BF16)


def _split2(x):
    hi = x.astype(BF16)
    return hi, (x - hi.astype(F32)).astype(BF16)


def _sigmoid(x):
    return 1.0 / (1.0 + jnp.exp(-x))


def _rms(x, g):
    return x * lax.rsqrt(jnp.mean(x * x, axis=-1, keepdims=True) + NORM_EPS) * g


def _seg_sum(x, ones_bd):
    hi, lo = _split2(x)
    return _dot(hi, ones_bd) + _dot(lo, ones_bd)


def _dot_split(x, w_hi, w_lo):
    x_hi, x_lo = _split2(x)
    return _dot(x_hi, w_hi) + _dot(x_hi, w_lo) + _dot(x_lo, w_hi)


def _mod_kernel(c_ref, w_ref, b_ref, o_ref):
    c = c_ref[...]
    s = c * _sigmoid(c)
    o_ref[...] = _dot(s, w_ref[...], HI) + b_ref[...]


def _modulation(c_all, w_ada, b_ada):
    rows = c_all.shape[0]
    n_blk = w_ada.shape[1] // D_MODEL
    return pl.pallas_call(
        _mod_kernel,
        grid=(n_blk,),
        in_specs=[pl.BlockSpec((rows, D_MODEL), lambda j: (0, 0)),
                  pl.BlockSpec((D_MODEL, D_MODEL), lambda j: (0, j)),
                  pl.BlockSpec((1, D_MODEL), lambda j: (0, j))],
        out_specs=pl.BlockSpec((rows, D_MODEL), lambda j: (0, j)),
        out_shape=jax.ShapeDtypeStruct((rows, w_ada.shape[1]), F32),
        name="adaln_mod",
    )(c_all, w_ada, b_ada)


def _inproj_kernel(x_ref, mod_ref, g_ref, w_ref, zc_ref, zr_ref):
    m = mod_ref[0]
    h = _rms(x_ref[0], g_ref[...]) * (1.0 + m[1:2]) + m[0:1]
    hb = h.astype(BF16)
    zc_ref[0] = _dot(hb, w_ref[:, :D_CONV_IN])
    zr_ref[0] = _dot(hb, w_ref[:, D_CONV_IN:])


def _inproj(x, mod, g_pre, w_in_bf):
    bsz, seq, _ = x.shape
    tm = TILE_IN
    d_in = w_in_bf.shape[1]
    return pl.pallas_call(
        _inproj_kernel,
        grid=(bsz, seq // tm),
        in_specs=[pl.BlockSpec((1, tm, D_MODEL), lambda b, i: (b, i, 0)),
                  pl.BlockSpec((1, 6, D_MODEL), lambda b, i: (b, 0, 0)),
                  pl.BlockSpec((1, D_MODEL), lambda b, i: (0, 0)),
                  pl.BlockSpec((D_MODEL, d_in), lambda b, i: (0, 0))],
        out_specs=[pl.BlockSpec((1, tm, D_CONV_IN), lambda b, i: (b, i, 0)),
                   pl.BlockSpec((1, tm, D_RWKV_IN), lambda b, i: (b, i, 0))],
        out_shape=[jax.ShapeDtypeStruct((bsz, seq, D_CONV_IN), F32),
                   jax.ShapeDtypeStruct((bsz, seq, D_RWKV_IN), F32)],
        compiler_params=pltpu.CompilerParams(
            dimension_semantics=("arbitrary", "arbitrary"), vmem_limit_bytes=VMEM_LIMIT),
        name="in_proj",
    )(x, mod, g_pre, w_in_bf)


def _time_masks(direction):
    sign = 1 if direction == 0 else -1
    row = lax.broadcasted_iota(jnp.int32, (CHUNK, PAIR), 0)
    lane = lax.broadcasted_iota(jnp.int32, (CHUNK, PAIR), 1)
    delta = (row - jnp.bitwise_and(lane, HEAD_DIM - 1)) * sign
    q_row = lax.broadcasted_iota(jnp.int32, (PAIR, PAIR), 0)
    q_lane = lax.broadcasted_iota(jnp.int32, (PAIR, PAIR), 1)
    return dict(strict=delta > 0, incl=delta >= 0,
                eye=jnp.where(delta == 0, 1.0, 0.0).astype(F32),
                lane_lo=lane < HEAD_DIM,
                bd=(q_row < HEAD_DIM) == (q_lane < HEAD_DIM))


def _block_diag(y, lane_lo):
    return _bf(jnp.concatenate([jnp.where(lane_lo, y, 0.0), jnp.where(lane_lo, 0.0, y)], axis=0))


def _chunk_affine(units, masks):
    lo = masks[0]["lane_lo"]
    bd = masks[0]["bd"]
    for u in units:
        u["aa"] = _dot_nt(_bf(jnp.concatenate([u["at"], u["rt"]], axis=0)),
                          jnp.concatenate([_block_diag(u["bt"], lo), _block_diag(u["kt"], lo)],
                                          axis=0))
    for u in units:
        m = masks[u["d"]]
        aa = u.pop("aa")
        u["n_ab"] = jnp.where(m["strict"], aa[:CHUNK, :PAIR], 0.0)
        u["a_ak"] = jnp.where(m["strict"], aa[:CHUNK, PAIR:], 0.0)
        u["a_rb"] = jnp.where(m["incl"], aa[CHUNK:, :PAIR], 0.0)
        u["a_rk"] = jnp.where(m["incl"], aa[CHUNK:, PAIR:], 0.0)
    for u in units:
        u["x"] = masks[u["d"]]["eye"] + u["n_ab"]
        u["pw"] = _dot(_bf(u["n_ab"]), _block_diag(u["n_ab"], lo))
    for u in units:
        av = _dot(_bf(jnp.concatenate([u["a_ak"], u["a_rk"]], axis=0)), _block_diag(u["v"], lo))
        u["akv"], u["rkv"] = av[:CHUNK], av[CHUNK:]
    width = 2
    while 2 * width < CHUNK:
        for u in units:
            zz = _dot(_bf(jnp.concatenate([u["x"], u["pw"]], axis=0)), _block_diag(u["pw"], lo))
            u["x"] = u["x"] + zz[:CHUNK]
            u["pw"] = zz[CHUNK:]
        width *= 2
    for u in units:
        u["x"] = u["x"] + _dot(_bf(u["x"]), _block_diag(u["pw"], lo))
    for u in units:
        wv = _dot(_bf(u["x"]), jnp.concatenate([_block_diag(u["at_true"], lo),
                                                _block_diag(u["akv"], lo)], axis=1))
        u["wt"], u["vt"] = wv[:, :PAIR], wv[:, PAIR:]
    for u in units:
        qy = _dot(_bf(u["a_rb"]), jnp.concatenate([_block_diag(u["wt"], lo),
                                                   _block_diag(u["vt"], lo)], axis=1))
        u["qhat"] = u["rt_true"] + qy[:, :PAIR]
        u["y0"] = qy[:, PAIR:] + u["rkv"]
    for u in units:
        u["mc"] = jnp.where(bd, _dot_tn(_bf(u["wt"]), _bf(u["bh"])), 0.0)
    for u in units:
        u["c0"] = jnp.where(bd, _dot_tn(_bf(jnp.concatenate([u["vt"], u["v"]], axis=0)),
                                        _bf(jnp.concatenate([u["bh"], u["kh"]], axis=0))), 0.0)


def _scan_prepare(direction, r, k, v, a, b, le, tmask):
    tt = r.shape[0]
    le_hi = le.astype(BF16)
    le_r = le - le_hi.astype(F32)
    le_mid = le_r.astype(BF16)
    le_lo = (le_r - le_mid.astype(F32)).astype(BF16)
    ct = _dot(tmask, le_hi) + _dot(tmask, le_mid) + _dot(tmask, le_lo)
    cum, tot = ct[:tt], ct[tt:]
    off = 0.5 * tot
    ci = cum - off
    e_off = jnp.exp(off)
    e_neg = jnp.exp(-ci)
    tile = dict(at=a * jnp.exp(ci - le), rt=r * jnp.exp(ci), bt=b * e_neg, kt=k * e_neg, v=v)
    tile["bh"] = tile["bt"] * e_off
    tile["kh"] = tile["kt"] * e_off
    tile["at_true"] = tile["at"] * e_off
    tile["rt_true"] = tile["rt"] * e_off
    units = {}
    for c in range(tt // CHUNK):
        rs = slice(c * CHUNK, (c + 1) * CHUNK)
        for p in range(N_PAIR):
            sl = slice(p * PAIR, (p + 1) * PAIR)
            units[c, p] = dict({name: t[rs, sl] for name, t in tile.items()}, d=direction)
    return units, jnp.exp(tot)


def _scan_tiles(scans, states):
    masks = [_time_masks(d) for d in range(len(scans))]
    units, decay = [], []
    for d, scan in enumerate(scans):
        u, dec = _scan_prepare(d, *scan)
        units.append(u)
        decay.append(dec)
    n_chunk = decay[0].shape[0] // CHUNK
    _chunk_affine([u for per_dir in units for u in per_dir.values()], masks)

    states = [list(s) for s in states]
    y_rows = [[None] * n_chunk for _ in scans]
    for j in range(n_chunk):
        for d in range(len(scans)):
            c = j if d == 0 else n_chunk - 1 - j
            ys = []
            for p in range(N_PAIR):
                u = units[d][c, p]
                s = states[d][p]
                s_bf = _bf(s)
                ys.append(_dot_nt(_bf(u["qhat"]), s_bf) + u["y0"])
                p_c = decay[d][c * CHUNK:c * CHUNK + 1, p * PAIR:(p + 1) * PAIR]
                states[d][p] = s * p_c + _dot(s_bf, _bf(u["mc"])) + u["c0"]
            y_rows[d][c] = jnp.concatenate(ys, axis=1)
    return [jnp.concatenate(rows, axis=0) for rows in y_rows], states


def _rwkv_pre(direction, z, edge, mu, w0, wup_hi, wup_lo, a0, aup_hi, aup_lo, gup_hi, gup_lo,
              k_k, k_a, ones_bd):
    tt = z.shape[0]
    rows = lax.broadcasted_iota(jnp.int32, (tt, 1), 0)
    if direction == 0:
        prev = jnp.where(rows == 0, edge, pltpu.roll(z, 1, 0))
    else:
        prev = jnp.where(rows == tt - 1, edge, pltpu.roll(z, tt - 1, 0))
    zs = z + mu * (prev - z)
    r = zs[:, 0:D_RWKV]
    k = zs[:, D_RWKV:2 * D_RWKV]
    v = zs[:, 2 * D_RWKV:3 * D_RWKV]
    lora = zs[:, 3 * D_RWKV:3 * D_RWKV + LORA_WA]
    gl = zs[:, 3 * D_RWKV + LORA_WA:]

    w_pre = w0 + _dot_split(jnp.tanh(lora), wup_hi, wup_lo)
    le = _sigmoid(w_pre) * DECAY_SCALE
    a_sig = _sigmoid(a0 + _dot_split(lora, aup_hi, aup_lo))
    g = _dot_split(_sigmoid(gl), gup_hi, gup_lo)
    kk = k * k_k
    kk = kk * lax.rsqrt(jnp.maximum(_seg_sum(kk * kk, ones_bd), 1e-24))
    k = k * (1.0 + (a_sig - 1.0) * k_a)
    return r, k, v, -kk, kk * a_sig, le, g


def _rwkv_post(y, r, k, v, g, r_k, ln_w, ln_b, ones_bd):
    mean = _seg_sum(y, ones_bd) * (1.0 / HEAD_DIM)
    yc = y - mean
    var = _seg_sum(yc * yc, ones_bd) * (1.0 / HEAD_DIM)
    yn = yc * lax.rsqrt(var + GN_EPS) * ln_w + ln_b
    bonus = _seg_sum(r * k * r_k, ones_bd) * v
    return (yn + bonus) * g


def _rwkv_kernel(zf_ref, hf_ref, zb_ref, hb_ref, mu_ref, w0_ref, wup_ref, a0_ref, aup_ref,
                 gup_ref, kk_ref, ka_ref, rk_ref, lnw_ref, lnb_ref, ones_ref, tmask_ref,
                 of_ref, ob_ref, s_ref):
    i = pl.program_id(1)

    @pl.when(i == 0)
    def _():
        s_ref[...] = jnp.zeros_like(s_ref)

    ones_bd = ones_ref[...]
    edges = (jnp.where(i > 0, hf_ref[0, HALO - 1:HALO, :], 0.0),
             jnp.where(i > 0, hb_ref[0, 0:1, :], 0.0))
    pre = [_rwkv_pre(d, z_ref[0], edges[d], mu_ref[d], w0_ref[d], wup_ref[d, 0], wup_ref[d, 1],
                     a0_ref[d], aup_ref[d, 0], aup_ref[d, 1], gup_ref[d, 0], gup_ref[d, 1],
                     kk_ref[d], ka_ref[d], ones_bd)
           for d, z_ref in enumerate((zf_ref, zb_ref))]
    states = [[s_ref[d, p] for p in range(N_PAIR)] for d in range(2)]
    ys, states = _scan_tiles([pre[d][:6] + (tmask_ref[d],) for d in range(2)], states)
    for d, o_ref in enumerate((of_ref, ob_ref)):
        r, k, v, _, _, _, g = pre[d]
        o_ref[0] = _rwkv_post(ys[d], r, k, v, g, rk_ref[d], lnw_ref[d], lnb_ref[d], ones_bd)
        for p in range(N_PAIR):
            s_ref[d, p] = states[d][p]


def _chunk_sum_masks(tt):
    t = np.arange(tt)
    same = (t[:, None] // CHUNK) == (t[None, :] // CHUNK)
    fwd = same & (t[None, :] <= t[:, None])
    bwd = same & (t[None, :] >= t[:, None])
    m = np.stack([np.concatenate([fwd, same], axis=0), np.concatenate([bwd, same], axis=0)])
    return jnp.asarray(m, dtype=BF16)


def _rwkv(z_rwkv, mu, w0, wup, a0, aup, gup, k_k, k_a, r_k, ln_w, ln_b, ones_bd):
    bsz, seq, _ = z_rwkv.shape
    tt = TILE_SCAN
    n_tile = seq // tt
    blk = tt // HALO
    last_blk = n_tile * blk - 1

    def whole(arr):
        return pl.BlockSpec(arr.shape, lambda b, i: (0,) * arr.ndim)

    tmask = _chunk_sum_masks(tt)
    params = (mu, w0, wup, a0, aup, gup, k_k, k_a, r_k, ln_w, ln_b, ones_bd, tmask)
    tile = (1, tt, D_RWKV_IN)
    halo = (1, HALO, D_RWKV_IN)
    out_tile = (1, tt, D_RWKV)
    return pl.pallas_call(
        _rwkv_kernel,
        grid=(bsz, n_tile),
        in_specs=[pl.BlockSpec(tile, lambda b, i: (b, i, 0)),
                  pl.BlockSpec(halo, lambda b, i: (b, jnp.maximum(i * blk - 1, 0), 0)),
                  pl.BlockSpec(tile, lambda b, i: (b, n_tile - 1 - i, 0)),
                  pl.BlockSpec(halo, lambda b, i: (b, jnp.minimum((n_tile - i) * blk, last_blk), 0)),
                  ] + [whole(p) for p in params],
        out_specs=[pl.BlockSpec(out_tile, lambda b, i: (b, i, 0)),
                   pl.BlockSpec(out_tile, lambda b, i: (b, n_tile - 1 - i, 0))],
        out_shape=[jax.ShapeDtypeStruct((bsz, seq, D_RWKV), F32)] * 2,
        scratch_shapes=[pltpu.VMEM((2, N_PAIR, PAIR, PAIR), F32)],
        compiler_params=pltpu.CompilerParams(
            dimension_semantics=("arbitrary", "arbitrary"), vmem_limit_bytes=VMEM_LIMIT),
        name="rwkv7_scan",
    )(z_rwkv, z_rwkv, z_rwkv, z_rwkv, *params)


def _out_kernel(x_ref, zc_ref, zp_ref, zn_ref, of_ref, ob_ref, mod_ref, cw_ref, g_post_mix_ref,
                g_pre_ffn_ref, g_post_ffn_ref, wout_ref, w1_ref, w2_ref, y_ref):
    i = pl.program_id(1)
    n = pl.num_programs(1)
    tm = x_ref.shape[1]
    m = mod_ref[0]
    x = x_ref[0]

    zc = zc_ref[0]
    gate_b = zc[:, 0:D_CONV]
    cu = zc[:, D_CONV:2 * D_CONV] * zc[:, 2 * D_CONV:]
    zp = zp_ref[0]
    zn = zn_ref[0]
    cu_prev = zp[HALO - 1:HALO, D_CONV:2 * D_CONV] * zp[HALO - 1:HALO, 2 * D_CONV:]
    cu_next = zn[0:1, D_CONV:2 * D_CONV] * zn[0:1, 2 * D_CONV:]
    cu_prev = jnp.where(i > 0, cu_prev, 0.0)
    cu_next = jnp.where(i < n - 1, cu_next, 0.0)
    rows = lax.broadcasted_iota(jnp.int32, (tm, 1), 0)
    before = jnp.where(rows == 0, cu_prev, pltpu.roll(cu, 1, 0))
    after = jnp.where(rows == tm - 1, cu_next, pltpu.roll(cu, tm - 1, 0))
    cw = cw_ref[...]
    y_conv = gate_b * (cw[0:1] * before + cw[1:2] * cu + cw[2:3] * after)

    y_rwkv = of_ref[0] + ob_ref[0]
    mix = (_dot(y_conv.astype(BF16), wout_ref[:D_CONV, :])
           + _dot(y_rwkv.astype(BF16), wout_ref[D_CONV:, :]))
    x1 = x + m[2:3] * _rms(mix, g_post_mix_ref[...])

    h = _rms(x1, g_pre_ffn_ref[...]) * (1.0 + m[4:5]) + m[3:4]
    f = _dot(h.astype(BF16), w1_ref[...])
    f = jnp.square(jnp.maximum(f, 0.0))
    f = _dot(f.astype(BF16), w2_ref[...])
    y_ref[0] = x1 + m[5:6] * _rms(f, g_post_ffn_ref[...])


def _outproj_ffn(x, z_conv, o_fwd, o_bwd, mod, conv_w_pad, g_post_mix, g_pre_ffn, g_post_ffn,
                 w_out_bf, w1_bf, w2_bf):
    bsz, seq, _ = x.shape
    tm = TILE_OUT
    n_tile = seq // tm
    blk = tm // HALO

    def const(shape):
        return pl.BlockSpec(shape, lambda b, i: (0,) * len(shape),
                            pipeline_mode=pl.Buffered(1))

    return pl.pallas_call(
        _out_kernel,
        grid=(bsz, n_tile),
        in_specs=[pl.BlockSpec((1, tm, D_MODEL), lambda b, i: (b, i, 0)),
                  pl.BlockSpec((1, tm, D_CONV_IN), lambda b, i: (b, i, 0)),
                  pl.BlockSpec((1, HALO, D_CONV_IN),
                               lambda b, i: (b, jnp.maximum(i * blk - 1, 0), 0)),
                  pl.BlockSpec((1, HALO, D_CONV_IN),
                               lambda b, i: (b, jnp.minimum((i + 1) * blk, n_tile * blk - 1), 0)),
                  pl.BlockSpec((1, tm, D_RWKV), lambda b, i: (b, i, 0)),
                  pl.BlockSpec((1, tm, D_RWKV), lambda b, i: (b, i, 0)),
                  pl.BlockSpec((1, 6, D_MODEL), lambda b, i: (b, 0, 0)),
                  const((HALO, D_CONV)), const((1, D_MODEL)), const((1, D_MODEL)),
                  const((1, D_MODEL)), const((D_MODEL, D_MODEL)), const((D_MODEL, D_FF)),
                  const((D_FF, D_MODEL))],
        out_specs=pl.BlockSpec((1, tm, D_MODEL), lambda b, i: (b, i, 0)),
        out_shape=jax.ShapeDtypeStruct((bsz, seq, D_MODEL), F32),
        compiler_params=pltpu.CompilerParams(
            dimension_semantics=("arbitrary", "arbitrary"), vmem_limit_bytes=VMEM_LIMIT),
        name="out_proj_ffn",
    )(x, z_conv, z_conv, z_conv, o_fwd, o_bwd, mod, conv_w_pad, g_post_mix, g_pre_ffn,
      g_post_ffn, w_out_bf, w1_bf, w2_bf)


def _lora_weight(w, top):
    w = jnp.pad(w, ((0, 0), (top, LORA_WA - top - w.shape[1]), (0, 0)))
    hi, lo = _split2(w)
    return jnp.stack([hi, lo], axis=1)


def _layer(xs, cs, w_ada, b_ada, g_pre_mix, g_post_mix, w_in, conv_w, mu_shift, w0, w_up, a0,
           a_up, g_up, k_k, k_a, r_k, ln_x_w, ln_x_b, w_out, g_pre_ffn, g_post_ffn, w_ffn1,
           w_ffn2):
    n_seq = sum(c.shape[0] for c in cs)
    c_all = jnp.pad(jnp.concatenate(cs, axis=0), ((0, HALO - n_seq), (0, 0)))
    mod = _modulation(c_all, w_ada, b_ada[None, :]).reshape(HALO, 6, D_MODEL)

    w_in_bf = w_in.astype(BF16)
    w_out_bf = w_out.astype(BF16)
    w1_bf = w_ffn1.astype(BF16)
    w2_bf = w_ffn2.astype(BF16)
    row = lambda p: p[None, :]
    dir_row = lambda p: p.reshape(2, 1, -1)
    wup = _lora_weight(w_up, 0)
    aup = _lora_weight(a_up, LORA_WA // 2)
    gup = _lora_weight(g_up, 0)
    conv_w_pad = jnp.pad(conv_w, ((0, HALO - conv_w.shape[0]), (0, 0)))
    head = jnp.arange(D_RWKV) // HEAD_DIM
    ones_bd = (head[:, None] == head[None, :]).astype(BF16)

    outs = []
    start = 0
    for x in xs:
        bsz = x.shape[0]
        m = mod[start:start + bsz]
        start += bsz
        z_conv, z_rwkv = _inproj(x, m, row(g_pre_mix), w_in_bf)
        o_fwd, o_bwd = _rwkv(z_rwkv, dir_row(mu_shift), dir_row(w0), wup, dir_row(a0), aup, gup,
                             dir_row(k_k), dir_row(k_a), dir_row(r_k), dir_row(ln_x_w),
                             dir_row(ln_x_b), ones_bd)
        outs.append(_outproj_ffn(x, z_conv, o_fwd, o_bwd, m, conv_w_pad, row(g_post_mix),
                                 row(g_pre_ffn), row(g_post_ffn), w_out_bf, w1_bf, w2_bf))
    return outs


def kernel(x_prompt, x_sample, c_prompt, c_sample, w_ada, b_ada, g_pre_mix, g_post_mix, w_in,
           conv_w, mu_shift, w0, w_up, a0, a_up, g_up, k_k, k_a, r_k, ln_x_w, ln_x_b, w_out,
           g_pre_ffn, g_post_ffn, w_ffn1, w_ffn2):
    xs = [x_prompt, x_sample]
    for l in range(w_ada.shape[0]):
        xs = _layer(xs, [c_prompt, c_sample], w_ada[l], b_ada[l], g_pre_mix[l], g_post_mix[l],
                    w_in[l], conv_w[l], mu_shift[l], w0[l], w_up[l], a0[l], a_up[l], g_up[l],
                    k_k[l], k_a[l], r_k[l], ln_x_w[l], ln_x_b[l], w_out[l], g_pre_ffn[l],
                    g_post_ffn[l], w_ffn1[l], w_ffn2[l])
    return (xs[0], xs[1])
```

```python
import numpy as np

import jax
import jax.numpy as jnp
from jax import lax
from jax.experimental import pallas as pl
from jax.experimental.pallas import tpu as pltpu

D_MODEL = 1024
D_CONV = 512
D_RWKV = 512
HEAD_DIM = 64
LORA_WA = 128
GATE_RANK = 128
D_RWKV_IN = 3 * D_RWKV + LORA_WA + GATE_RANK
D_CONV_IN = 3 * D_CONV
D_FF = 4096
NORM_EPS = 1e-6
GN_EPS = 64e-5
DECAY_SCALE = -0.6065306597126334

CHUNK = 64
PAIR = 2 * HEAD_DIM
N_PAIR = D_RWKV // PAIR
HALO = 8

TILE_IN = 256
TILE_SCAN = 256
TILE_OUT = 256
VMEM_LIMIT = 56 * 1024 * 1024

F32 = jnp.float32
BF16 = jnp.bfloat16
HI = lax.Precision.HIGHEST


def _dot(a, b, precision=None):
    return jnp.dot(a, b, precision=precision, preferred_element_type=F32)


def _dot_nt(a, b):
    return lax.dot_general(a, b, (((1,), (1,)), ((), ())), preferred_element_type=F32)


def _dot_tn(a, b):
    return lax.dot_general(a, b, (((0,), (0,)), ((), ())), preferred_element_type=F32)


def _bf(x):
    return x.astype(BF16)


def _split2(x):
    hi = x.astype(BF16)
    return hi, (x - hi.astype(F32)).astype(BF16)


def _sigmoid(x):
    return 1.0 / (1.0 + jnp.exp(-x))


def _rms(x, g):
    return x * lax.rsqrt(jnp.mean(x * x, axis=-1, keepdims=True) + NORM_EPS) * g


def _seg_sum(x, ones_bd):
    return _dot(_bf(x), ones_bd)


def _mod_kernel(c_ref, w_ref, b_ref, o_ref):
    c = c_ref[...]
    s = c * _sigmoid(c)
    o_ref[...] = _dot(s, w_ref[...], HI) + b_ref[...]


def _modulation(c_all, w_ada, b_ada):
    rows = c_all.shape[0]
    n_blk = w_ada.shape[1] // D_MODEL
    return pl.pallas_call(
        _mod_kernel,
        grid=(n_blk,),
        in_specs=[pl.BlockSpec((rows, D_MODEL), lambda j: (0, 0)),
                  pl.BlockSpec((D_MODEL, D_MODEL), lambda j: (0, j)),
                  pl.BlockSpec((1, D_MODEL), lambda j: (0, j))],
        out_specs=pl.BlockSpec((rows, D_MODEL), lambda j: (0, j)),
        out_shape=jax.ShapeDtypeStruct((rows, w_ada.shape[1]), F32),
        name="adaln_mod",
    )(c_all, w_ada, b_ada)


def _inproj_kernel(x_ref, mod_ref, g_ref, w_ref, zc_ref, zr_ref):
    m = mod_ref[0]
    h = _rms(x_ref[0], g_ref[...]) * (1.0 + m[1:2]) + m[0:1]
    hb = h.astype(BF16)
    zc_ref[0] = _dot(hb, w_ref[:, :D_CONV_IN])
    zr_ref[0] = _dot(hb, w_ref[:, D_CONV_IN:])


def _inproj(x, mod, g_pre, w_in_bf):
    bsz, seq, _ = x.shape
    tm = TILE_IN
    d_in = w_in_bf.shape[1]
    return pl.pallas_call(
        _inproj_kernel,
        grid=(bsz, seq // tm),
        in_specs=[pl.BlockSpec((1, tm, D_MODEL), lambda b, i: (b, i, 0)),
                  pl.BlockSpec((1, 6, D_MODEL), lambda b, i: (b, 0, 0)),
                  pl.BlockSpec((1, D_MODEL), lambda b, i: (0, 0)),
                  pl.BlockSpec((D_MODEL, d_in), lambda b, i: (0, 0))],
        out_specs=[pl.BlockSpec((1, tm, D_CONV_IN), lambda b, i: (b, i, 0)),
                   pl.BlockSpec((1, tm, D_RWKV_IN), lambda b, i: (b, i, 0))],
        out_shape=[jax.ShapeDtypeStruct((bsz, seq, D_CONV_IN), F32),
                   jax.ShapeDtypeStruct((bsz, seq, D_RWKV_IN), F32)],
        compiler_params=pltpu.CompilerParams(
            dimension_semantics=("arbitrary", "arbitrary"), vmem_limit_bytes=VMEM_LIMIT),
        name="in_proj",
    )(x, mod, g_pre, w_in_bf)


def _time_masks(direction):
    sign = 1 if direction == 0 else -1
    row = lax.broadcasted_iota(jnp.int32, (CHUNK, PAIR), 0)
    lane = lax.broadcasted_iota(jnp.int32, (CHUNK, PAIR), 1)
    delta = (row - jnp.bitwise_and(lane, HEAD_DIM - 1)) * sign
    q_row = lax.broadcasted_iota(jnp.int32, (PAIR, PAIR), 0)
    q_lane = lax.broadcasted_iota(jnp.int32, (PAIR, PAIR), 1)
    return dict(strict=delta > 0, incl=delta >= 0,
                eye=jnp.where(delta == 0, 1.0, 0.0).astype(F32),
                lane_lo=lane < HEAD_DIM,
                bd=(q_row < HEAD_DIM) == (q_lane < HEAD_DIM))


def _block_diag(y, lane_lo):
    return _bf(jnp.concatenate([jnp.where(lane_lo, y, 0.0), jnp.where(lane_lo, 0.0, y)], axis=0))


def _chunk_affine(units, masks):
    lo = masks[0]["lane_lo"]
    bd = masks[0]["bd"]
    for u in units:
        u["aa"] = _dot_nt(_bf(jnp.concatenate([u["at"], u["rt"]], axis=0)),
                          jnp.concatenate([_block_diag(u["bt"], lo), _block_diag(u["kt"], lo)],
                                          axis=0))
    for u in units:
        m = masks[u["d"]]
        aa = u.pop("aa")
        u["n_ab"] = jnp.where(m["strict"], aa[:CHUNK, :PAIR], 0.0)
        u["a_ak"] = jnp.where(m["strict"], aa[:CHUNK, PAIR:], 0.0)
        u["a_rb"] = jnp.where(m["incl"], aa[CHUNK:, :PAIR], 0.0)
        u["a_rk"] = jnp.where(m["incl"], aa[CHUNK:, PAIR:], 0.0)
    for u in units:
        u["x"] = masks[u["d"]]["eye"] + u["n_ab"]
        u["pw"] = _dot(_bf(u["n_ab"]), _block_diag(u["n_ab"], lo))
    for u in units:
        av = _dot(_bf(jnp.concatenate([u["a_ak"], u["a_rk"]], axis=0)), _block_diag(u["v"], lo))
        u["akv"], u["rkv"] = av[:CHUNK], av[CHUNK:]
    width = 2
    while 2 * width < CHUNK:
        for u in units:
            zz = _dot(_bf(u["pw"]), jnp.concatenate([_block_diag(u["pw"], lo),
                                                     _block_diag(u["x"], lo)], axis=1))
            u["pw"] = zz[:, :PAIR]
            u["x"] = u["x"] + zz[:, PAIR:]
        width *= 2
    for u in units:
        u["x"] = u["x"] + _dot(_bf(u["pw"]), _block_diag(u["x"], lo))
    for u in units:
        wv = _dot(_bf(u["x"]), jnp.concatenate([_block_diag(u["at_true"], lo),
                                                _block_diag(u["akv"], lo)], axis=1))
        u["wt"], u["vt"] = wv[:, :PAIR], wv[:, PAIR:]
    for u in units:
        qy = _dot(_bf(u["a_rb"]), jnp.concatenate([_block_diag(u["wt"], lo),
                                                   _block_diag(u["vt"], lo)], axis=1))
        u["qhat"] = u["rt_true"] + qy[:, :PAIR]
        u["y0"] = qy[:, PAIR:] + u["rkv"]
    for u in units:
        u["mc"] = jnp.where(bd, _dot_tn(_bf(u["wt"]), _bf(u["bh"])), 0.0)
    for u in units:
        u["c0"] = jnp.where(bd, _dot_tn(_bf(jnp.concatenate([u["vt"], u["v"]], axis=0)),
                                        _bf(jnp.concatenate([u["bh"], u["kh"]], axis=0))), 0.0)


def _scan_prepare(direction, r, k, v, a, b, le, tmask):
    tt = r.shape[0]
    le_hi, le_lo = _split2(le)
    cum = _dot(tmask, le_hi) + _dot(tmask, le_lo)
    last = CHUNK - 1 if direction == 0 else 0
    tot = jnp.concatenate(
        [jnp.broadcast_to(cum[c * CHUNK + last:c * CHUNK + last + 1], (CHUNK, D_RWKV))
         for c in range(tt // CHUNK)], axis=0)
    off = 0.5 * tot
    ci = cum - off
    e_off = jnp.exp(off)
    e_neg = jnp.exp(-ci)
    tile = dict(at=a * jnp.exp(ci - le), rt=r * jnp.exp(ci), bt=b * e_neg, kt=k * e_neg, v=v)
    tile["bh"] = tile["bt"] * e_off
    tile["kh"] = tile["kt"] * e_off
    tile["at_true"] = tile["at"] * e_off
    tile["rt_true"] = tile["rt"] * e_off
    units = {}
    for c in range(tt // CHUNK):
        rs = slice(c * CHUNK, (c + 1) * CHUNK)
        for p in range(N_PAIR):
            sl = slice(p * PAIR, (p + 1) * PAIR)
            units[c, p] = dict({name: t[rs, sl] for name, t in tile.items()}, d=direction)
    return units, jnp.exp(tot)


def _scan_tiles(scans, states):
    masks = [_time_masks(d) for d in range(len(scans))]
    units, decay = [], []
    for d, scan in enumerate(scans):
        u, dec = _scan_prepare(d, *scan)
        units.append(u)
        decay.append(dec)
    n_chunk = decay[0].shape[0] // CHUNK
    _chunk_affine([u for per_dir in units for u in per_dir.values()], masks)

    states = [list(s) for s in states]
    y_rows = [[None] * n_chunk for _ in scans]
    for j in range(n_chunk):
        for d in range(len(scans)):
            c = j if d == 0 else n_chunk - 1 - j
            ys = []
            for p in range(N_PAIR):
                u = units[d][c, p]
                s = states[d][p]
                s_bf = _bf(s)
                ys.append(_dot_nt(_bf(u["qhat"]), s_bf) + u["y0"])
                p_c = decay[d][c * CHUNK:c * CHUNK + 1, p * PAIR:(p + 1) * PAIR]
                states[d][p] = s * p_c + _dot(s_bf, _bf(u["mc"])) + u["c0"]
            y_rows[d][c] = jnp.concatenate(ys, axis=1)
    return [jnp.concatenate(rows, axis=0) for rows in y_rows], states


def _rwkv_pre(direction, z, edge, mu, w0, wup, a0, aup, gup, k_k, k_a, ones_bd):
    tt = z.shape[0]
    rows = lax.broadcasted_iota(jnp.int32, (tt, 1), 0)
    if direction == 0:
        prev = jnp.where(rows == 0, edge, pltpu.roll(z, 1, 0))
    else:
        prev = jnp.where(rows == tt - 1, edge, pltpu.roll(z, tt - 1, 0))
    zs = z + mu * (prev - z)
    r = zs[:, 0:D_RWKV]
    k = zs[:, D_RWKV:2 * D_RWKV]
    v = zs[:, 2 * D_RWKV:3 * D_RWKV]
    lora = zs[:, 3 * D_RWKV:3 * D_RWKV + LORA_WA]
    gl = zs[:, 3 * D_RWKV + LORA_WA:]

    w_pre = w0 + _dot(_bf(jnp.tanh(lora)), wup)
    le = _sigmoid(w_pre) * DECAY_SCALE
    a_sig = _sigmoid(a0 + _dot(_bf(lora), aup))
    g = _dot(_bf(_sigmoid(gl)), gup)
    kk = k * k_k
    kk = kk * lax.rsqrt(jnp.maximum(_seg_sum(kk * kk, ones_bd), 1e-24))
    k = k * (1.0 + (a_sig - 1.0) * k_a)
    return r, k, v, -kk, kk * a_sig, le, g


def _rwkv_post(y, r, k, v, g, r_k, ln_w, ln_b, ones_bd):
    mean = _seg_sum(y, ones_bd) * (1.0 / HEAD_DIM)
    yc = y - mean
    var = _seg_sum(yc * yc, ones_bd) * (1.0 / HEAD_DIM)
    yn = yc * lax.rsqrt(var + GN_EPS) * ln_w + ln_b
    bonus = _seg_sum(r * k * r_k, ones_bd) * v
    return (yn + bonus) * g


def _rwkv_kernel(zf_ref, hf_ref, zb_ref, hb_ref, mu_ref, w0_ref, wup_ref, a0_ref, aup_ref,
                 gup_ref, kk_ref, ka_ref, rk_ref, lnw_ref, lnb_ref, ones_ref, tmask_ref,
                 of_ref, ob_ref, s_ref):
    i = pl.program_id(1)

    @pl.when(i == 0)
    def _():
        s_ref[...] = jnp.zeros_like(s_ref)

    ones_bd = ones_ref[...]
    edges = (jnp.where(i > 0, hf_ref[0, HALO - 1:HALO, :], 0.0),
             jnp.where(i > 0, hb_ref[0, 0:1, :], 0.0))
    pre = [_rwkv_pre(d, z_ref[0], edges[d], mu_ref[d], w0_ref[d], wup_ref[d], a0_ref[d],
                     aup_ref[d], gup_ref[d], kk_ref[d], ka_ref[d], ones_bd)
           for d, z_ref in enumerate((zf_ref, zb_ref))]
    states = [[s_ref[d, p] for p in range(N_PAIR)] for d in range(2)]
    ys, states = _scan_tiles([pre[d][:6] + (tmask_ref[d],) for d in range(2)], states)
    for d, o_ref in enumerate((of_ref, ob_ref)):
        r, k, v, _, _, _, g = pre[d]
        o_ref[0] = _rwkv_post(ys[d], r, k, v, g, rk_ref[d], lnw_ref[d], lnb_ref[d], ones_bd)
        for p in range(N_PAIR):
            s_ref[d, p] = states[d][p]


def _chunk_sum_masks(tt):
    t = np.arange(tt)
    same = (t[:, None] // CHUNK) == (t[None, :] // CHUNK)
    fwd = same & (t[None, :] <= t[:, None])
    bwd = same & (t[None, :] >= t[:, None])
    return jnp.asarray(np.stack([fwd, bwd]), dtype=BF16)


def _rwkv(z_rwkv, mu, w0, wup, a0, aup, gup, k_k, k_a, r_k, ln_w, ln_b, ones_bd):
    bsz, seq, _ = z_rwkv.shape
    tt = TILE_SCAN
    n_tile = seq // tt
    blk = tt // HALO
    last_blk = n_tile * blk - 1

    def whole(arr):
        return pl.BlockSpec(arr.shape, lambda b, i: (0,) * arr.ndim)

    tmask = _chunk_sum_masks(tt)
    params = (mu, w0, wup, a0, aup, gup, k_k, k_a, r_k, ln_w, ln_b, ones_bd, tmask)
    tile = (1, tt, D_RWKV_IN)
    halo = (1, HALO, D_RWKV_IN)
    out_tile = (1, tt, D_RWKV)
    return pl.pallas_call(
        _rwkv_kernel,
        grid=(bsz, n_tile),
        in_specs=[pl.BlockSpec(tile, lambda b, i: (b, i, 0)),
                  pl.BlockSpec(halo, lambda b, i: (b, jnp.maximum(i * blk - 1, 0), 0)),
                  pl.BlockSpec(tile, lambda b, i: (b, n_tile - 1 - i, 0)),
                  pl.BlockSpec(halo, lambda b, i: (b, jnp.minimum((n_tile - i) * blk, last_blk), 0)),
                  ] + [whole(p) for p in params],
        out_specs=[pl.BlockSpec(out_tile, lambda b, i: (b, i, 0)),
                   pl.BlockSpec(out_tile, lambda b, i: (b, n_tile - 1 - i, 0))],
        out_shape=[jax.ShapeDtypeStruct((bsz, seq, D_RWKV), F32)] * 2,
        scratch_shapes=[pltpu.VMEM((2, N_PAIR, PAIR, PAIR), F32)],
        compiler_params=pltpu.CompilerParams(
            dimension_semantics=("arbitrary", "arbitrary"), vmem_limit_bytes=VMEM_LIMIT),
        name="rwkv7_scan",
    )(z_rwkv, z_rwkv, z_rwkv, z_rwkv, *params)


def _out_kernel(x_ref, zc_ref, zp_ref, zn_ref, of_ref, ob_ref, mod_ref, cw_ref, g_post_mix_ref,
                g_pre_ffn_ref, g_post_ffn_ref, wout_ref, w1_ref, w2_ref, y_ref):
    i = pl.program_id(1)
    n = pl.num_programs(1)
    tm = x_ref.shape[1]
    m = mod_ref[0]
    x = x_ref[0]

    zc = zc_ref[0]
    gate_b = zc[:, 0:D_CONV]
    cu = zc[:, D_CONV:2 * D_CONV] * zc[:, 2 * D_CONV:]
    zp = zp_ref[0]
    zn = zn_ref[0]
    cu_prev = zp[HALO - 1:HALO, D_CONV:2 * D_CONV] * zp[HALO - 1:HALO, 2 * D_CONV:]
    cu_next = zn[0:1, D_CONV:2 * D_CONV] * zn[0:1, 2 * D_CONV:]
    cu_prev = jnp.where(i > 0, cu_prev, 0.0)
    cu_next = jnp.where(i < n - 1, cu_next, 0.0)
    rows = lax.broadcasted_iota(jnp.int32, (tm, 1), 0)
    before = jnp.where(rows == 0, cu_prev, pltpu.roll(cu, 1, 0))
    after = jnp.where(rows == tm - 1, cu_next, pltpu.roll(cu, tm - 1, 0))
    cw = cw_ref[...]
    y_conv = gate_b * (cw[0:1] * before + cw[1:2] * cu + cw[2:3] * after)

    y_rwkv = of_ref[0] + ob_ref[0]
    mix = (_dot(y_conv.astype(BF16), wout_ref[:D_CONV, :])
           + _dot(y_rwkv.astype(BF16), wout_ref[D_CONV:, :]))
    x1 = x + m[2:3] * _rms(mix, g_post_mix_ref[...])

    h = _rms(x1, g_pre_ffn_ref[...]) * (1.0 + m[4:5]) + m[3:4]
    f = _dot(h.astype(BF16), w1_ref[...])
    f = jnp.square(jnp.maximum(f, 0.0))
    f = _dot(f.astype(BF16), w2_ref[...])
    y_ref[0] = x1 + m[5:6] * _rms(f, g_post_ffn_ref[...])


def _outproj_ffn(x, z_conv, o_fwd, o_bwd, mod, conv_w_pad, g_post_mix, g_pre_ffn, g_post_ffn,
                 w_out_bf, w1_bf, w2_bf):
    bsz, seq, _ = x.shape
    tm = TILE_OUT
    n_tile = seq // tm
    blk = tm // HALO

    def const(shape):
        return pl.BlockSpec(shape, lambda b, i: (0,) * len(shape),
                            pipeline_mode=pl.Buffered(1))

    return pl.pallas_call(
        _out_kernel,
        grid=(bsz, n_tile),
        in_specs=[pl.BlockSpec((1, tm, D_MODEL), lambda b, i: (b, i, 0)),
                  pl.BlockSpec((1, tm, D_CONV_IN), lambda b, i: (b, i, 0)),
                  pl.BlockSpec((1, HALO, D_CONV_IN),
                               lambda b, i: (b, jnp.maximum(i * blk - 1, 0), 0)),
                  pl.BlockSpec((1, HALO, D_CONV_IN),
                               lambda b, i: (b, jnp.minimum((i + 1) * blk, n_tile * blk - 1), 0)),
                  pl.BlockSpec((1, tm, D_RWKV), lambda b, i: (b, i, 0)),
                  pl.BlockSpec((1, tm, D_RWKV), lambda b, i: (b, i, 0)),
                  pl.BlockSpec((1, 6, D_MODEL), lambda b, i: (b, 0, 0)),
                  const((HALO, D_CONV)), const((1, D_MODEL)), const((1, D_MODEL)),
                  const((1, D_MODEL)), const((D_MODEL, D_MODEL)), const((D_MODEL, D_FF)),
                  const((D_FF, D_MODEL))],
        out_specs=pl.BlockSpec((1, tm, D_MODEL), lambda b, i: (b, i, 0)),
        out_shape=jax.ShapeDtypeStruct((bsz, seq, D_MODEL), F32),
        compiler_params=pltpu.CompilerParams(
            dimension_semantics=("arbitrary", "arbitrary"), vmem_limit_bytes=VMEM_LIMIT),
        name="out_proj_ffn",
    )(x, z_conv, z_conv, z_conv, o_fwd, o_bwd, mod, conv_w_pad, g_post_mix, g_pre_ffn,
      g_post_ffn, w_out_bf, w1_bf, w2_bf)


def _lora_weight(w, top):
    return _bf(jnp.pad(w, ((0, 0), (top, LORA_WA - top - w.shape[1]), (0, 0))))


def _layer(xs, cs, w_ada, b_ada, g_pre_mix, g_post_mix, w_in, conv_w, mu_shift, w0, w_up, a0,
           a_up, g_up, k_k, k_a, r_k, ln_x_w, ln_x_b, w_out, g_pre_ffn, g_post_ffn, w_ffn1,
           w_ffn2):
    n_seq = sum(c.shape[0] for c in cs)
    c_all = jnp.pad(jnp.concatenate(cs, axis=0), ((0, HALO - n_seq), (0, 0)))
    mod = _modulation(c_all, w_ada, b_ada[None, :]).reshape(HALO, 6, D_MODEL)

    w_in_bf = w_in.astype(BF16)
    w_out_bf = w_out.astype(BF16)
    w1_bf = w_ffn1.astype(BF16)
    w2_bf = w_ffn2.astype(BF16)
    row = lambda p: p[None, :]
    dir_row = lambda p: p.reshape(2, 1, -1)
    wup = _lora_weight(w_up, 0)
    aup = _lora_weight(a_up, LORA_WA // 2)
    gup = _lora_weight(g_up, 0)
    conv_w_pad = jnp.pad(conv_w, ((0, HALO - conv_w.shape[0]), (0, 0)))
    head = jnp.arange(D_RWKV) // HEAD_DIM
    ones_bd = (head[:, None] == head[None, :]).astype(BF16)

    outs = []
    start = 0
    for x in xs:
        bsz = x.shape[0]
        m = mod[start:start + bsz]
        start += bsz
        z_conv, z_rwkv = _inproj(x, m, row(g_pre_mix), w_in_bf)
        o_fwd, o_bwd = _rwkv(z_rwkv, dir_row(mu_shift), dir_row(w0), wup, dir_row(a0), aup, gup,
                             dir_row(k_k), dir_row(k_a), dir_row(r_k), dir_row(ln_x_w),
                             dir_row(ln_x_b), ones_bd)
        outs.append(_outproj_ffn(x, z_conv, o_fwd, o_bwd, m, conv_w_pad, row(g_post_mix),
                                 row(g_pre_ffn), row(g_post_ffn), w_out_bf, w1_bf, w2_bf))
    return outs


def kernel(x_prompt, x_sample, c_prompt, c_sample, w_ada, b_ada, g_pre_mix, g_post_mix, w_in,
           conv_w, mu_shift, w0, w_up, a0, a_up, g_up, k_k, k_a, r_k, ln_x_w, ln_x_b, w_out,
           g_pre_ffn, g_post_ffn, w_ffn1, w_ffn2):
    xs = [x_prompt, x_sample]
    for l in range(w_ada.shape[0]):
        xs = _layer(xs, [c_prompt, c_sample], w_ada[l], b_ada[l], g_pre_mix[l], g_post_mix[l],
                    w_in[l], conv_w[l], mu_shift[l], w0[l], w_up[l], a0[l], a_up[l], g_up[l],
                    k_k[l], k_a[l], r_k[l], ln_x_w[l], ln_x_b[l], w_out[l], g_pre_ffn[l],
                    g_post_ffn[l], w_ffn1[l], w_ffn2[l])
    return (xs[0], xs[1])
```

```python
import numpy as np

import jax
import jax.numpy as jnp
from jax import lax
from jax.experimental import pallas as pl
from jax.experimental.pallas import tpu as pltpu

D_MODEL = 1024
D_CONV = 512
D_RWKV = 512
HEAD_DIM = 64
LORA_WA = 128
GATE_RANK = 128
D_RWKV_IN = 3 * D_RWKV + LORA_WA + GATE_RANK
D_CONV_IN = 3 * D_CONV
D_FF = 4096
NORM_EPS = 1e-6
GN_EPS = 64e-5
DECAY_SCALE = -0.6065306597126334

CHUNK = 64
PAIR = 2 * HEAD_DIM
N_PAIR = D_RWKV // PAIR
HALO = 8
SEG_LANES = 256

TILE_IN = 512
TILE_SCAN = 256
TILE_OUT = 512
FF_CHUNK = 1024
VMEM_LIMIT = 56 * 1024 * 1024

F32 = jnp.float32
BF16 = jnp.bfloat16
HI = lax.Precision.HIGHEST


def _dot(a, b, precision=None):
    return jnp.dot(a, b, precision=precision, preferred_element_type=F32)


def _dot_nt(a, b):
    return lax.dot_general(a, b, (((1,), (1,)), ((), ())), preferred_element_type=F32)


def _dot_tn(a, b):
    return lax.dot_general(a, b, (((0,), (0,)), ((), ())), preferred_element_type=F32)


def _bf(x):
    return x.astype(BF16)


def _split2(x):
    hi = x.astype(BF16)
    return hi, (x - hi.astype(F32)).astype(BF16)


def _sigmoid(x):
    return 1.0 / (1.0 + jnp.exp(-x))


def _rms(x, g):
    return x * lax.rsqrt(jnp.mean(x * x, axis=-1, keepdims=True) + NORM_EPS) * g


def _seg_sum(x, ones_bd):
    xb = _bf(x)
    return jnp.concatenate([_dot(xb[:, j:j + SEG_LANES], ones_bd)
                            for j in range(0, x.shape[1], SEG_LANES)], axis=1)


def _mod_kernel(c_ref, w_ref, b_ref, o_ref):
    c = c_ref[...]
    s = c * _sigmoid(c)
    o_ref[...] = _dot(s, w_ref[...], HI) + b_ref[...]


def _modulation(c_all, w_ada, b_ada):
    rows = c_all.shape[0]
    n_blk = w_ada.shape[1] // D_MODEL
    return pl.pallas_call(
        _mod_kernel,
        grid=(n_blk,),
        in_specs=[pl.BlockSpec((rows, D_MODEL), lambda j: (0, 0)),
                  pl.BlockSpec((D_MODEL, D_MODEL), lambda j: (0, j)),
                  pl.BlockSpec((1, D_MODEL), lambda j: (0, j))],
        out_specs=pl.BlockSpec((rows, D_MODEL), lambda j: (0, j)),
        out_shape=jax.ShapeDtypeStruct((rows, w_ada.shape[1]), F32),
        name="adaln_mod",
    )(c_all, w_ada, b_ada)


def _inproj_kernel(x_ref, mod_ref, g_ref, w_ref, zc_ref, zr_ref):
    m = mod_ref[0]
    h = _rms(x_ref[0], g_ref[...]) * (1.0 + m[1:2]) + m[0:1]
    hb = h.astype(BF16)
    zc_ref[0] = _dot(hb, w_ref[:, :D_CONV_IN])
    zr_ref[0] = _dot(hb, w_ref[:, D_CONV_IN:])


def _inproj(x, mod, g_pre, w_in_bf):
    bsz, seq, _ = x.shape
    tm = TILE_IN
    d_in = w_in_bf.shape[1]
    return pl.pallas_call(
        _inproj_kernel,
        grid=(bsz, seq // tm),
        in_specs=[pl.BlockSpec((1, tm, D_MODEL), lambda b, i: (b, i, 0)),
                  pl.BlockSpec((1, 6, D_MODEL), lambda b, i: (b, 0, 0)),
                  pl.BlockSpec((1, D_MODEL), lambda b, i: (0, 0)),
                  pl.BlockSpec((D_MODEL, d_in), lambda b, i: (0, 0),
                               pipeline_mode=pl.Buffered(1))],
        out_specs=[pl.BlockSpec((1, tm, D_CONV_IN), lambda b, i: (b, i, 0)),
                   pl.BlockSpec((1, tm, D_RWKV_IN), lambda b, i: (b, i, 0))],
        out_shape=[jax.ShapeDtypeStruct((bsz, seq, D_CONV_IN), F32),
                   jax.ShapeDtypeStruct((bsz, seq, D_RWKV_IN), F32)],
        compiler_params=pltpu.CompilerParams(
            dimension_semantics=("arbitrary", "arbitrary"), vmem_limit_bytes=VMEM_LIMIT),
        name="in_proj",
    )(x, mod, g_pre, w_in_bf)


def _time_masks(direction):
    sign = 1 if direction == 0 else -1
    row = lax.broadcasted_iota(jnp.int32, (CHUNK, PAIR), 0)
    lane = lax.broadcasted_iota(jnp.int32, (CHUNK, PAIR), 1)
    delta = (row - jnp.bitwise_and(lane, HEAD_DIM - 1)) * sign
    q_row = lax.broadcasted_iota(jnp.int32, (PAIR, PAIR), 0)
    q_lane = lax.broadcasted_iota(jnp.int32, (PAIR, PAIR), 1)
    return dict(strict=delta > 0, incl=delta >= 0,
                eye=jnp.where(delta == 0, 1.0, 0.0).astype(F32),
                lane_lo=lane < HEAD_DIM,
                bd=(q_row < HEAD_DIM) == (q_lane < HEAD_DIM))


def _block_diag(y, lane_lo):
    return _bf(jnp.concatenate([jnp.where(lane_lo, y, 0.0), jnp.where(lane_lo, 0.0, y)], axis=0))


def _chunk_affine(units, masks):
    lo = masks[0]["lane_lo"]
    bd = masks[0]["bd"]
    for u in units:
        u["aa"] = _dot_nt(_bf(jnp.concatenate([u["at"], u["rt"]], axis=0)),
                          jnp.concatenate([_block_diag(u["bt"], lo), _block_diag(u["kt"], lo)],
                                          axis=0))
    for u in units:
        m = masks[u["d"]]
        aa = u.pop("aa")
        u["n_ab"] = jnp.where(m["strict"], aa[:CHUNK, :PAIR], 0.0)
        u["a_ak"] = jnp.where(m["strict"], aa[:CHUNK, PAIR:], 0.0)
        u["a_rb"] = jnp.where(m["incl"], aa[CHUNK:, :PAIR], 0.0)
        u["a_rk"] = jnp.where(m["incl"], aa[CHUNK:, PAIR:], 0.0)
    for u in units:
        u["x"] = masks[u["d"]]["eye"] + u["n_ab"]
        u["pw"] = _dot(_bf(u["n_ab"]), _block_diag(u["n_ab"], lo))
    for u in units:
        av = _dot(_bf(jnp.concatenate([u["a_ak"], u["a_rk"]], axis=0)), _block_diag(u["v"], lo))
        u["akv"], u["rkv"] = av[:CHUNK], av[CHUNK:]
    width = 2
    while 2 * width < CHUNK:
        for u in units:
            zz = _dot(_bf(u["pw"]), jnp.concatenate([_block_diag(u["pw"], lo),
                                                     _block_diag(u["x"], lo)], axis=1))
            u["pw"] = zz[:, :PAIR]
            u["x"] = u["x"] + zz[:, PAIR:]
        width *= 2
    for u in units:
        u["x"] = u["x"] + _dot(_bf(u["pw"]), _block_diag(u["x"], lo))
    for u in units:
        wv = _dot(_bf(u["x"]), jnp.concatenate([_block_diag(u["at_true"], lo),
                                                _block_diag(u["akv"], lo)], axis=1))
        u["wt"], u["vt"] = wv[:, :PAIR], wv[:, PAIR:]
    for u in units:
        qy = _dot(_bf(u["a_rb"]), jnp.concatenate([_block_diag(u["wt"], lo),
                                                   _block_diag(u["vt"], lo)], axis=1))
        u["qhat"] = u["rt_true"] + qy[:, :PAIR]
        u["y0"] = qy[:, PAIR:] + u["rkv"]
    for u in units:
        u["mc"] = jnp.where(bd, _dot_tn(_bf(u["wt"]), _bf(u["bh"])), 0.0)
    for u in units:
        u["c0"] = jnp.where(bd, _dot_tn(_bf(jnp.concatenate([u["vt"], u["v"]], axis=0)),
                                        _bf(jnp.concatenate([u["bh"], u["kh"]], axis=0))), 0.0)


def _scan_prepare(direction, r, k, v, a, b, le, tmask):
    tt = r.shape[0]
    le_hi, le_lo = _split2(le)
    cum = _dot(tmask, le_hi) + _dot(tmask, le_lo)
    last = CHUNK - 1 if direction == 0 else 0
    tot = jnp.concatenate(
        [jnp.broadcast_to(cum[c * CHUNK + last:c * CHUNK + last + 1], (CHUNK, D_RWKV))
         for c in range(tt // CHUNK)], axis=0)
    off = 0.5 * tot
    ci = cum - off
    e_off = jnp.exp(off)
    e_neg = jnp.exp(-ci)
    tile = dict(at=a * jnp.exp(ci - le), rt=r * jnp.exp(ci), bt=b * e_neg, kt=k * e_neg, v=v)
    tile["bh"] = tile["bt"] * e_off
    tile["kh"] = tile["kt"] * e_off
    tile["at_true"] = tile["at"] * e_off
    tile["rt_true"] = tile["rt"] * e_off
    units = {}
    for c in range(tt // CHUNK):
        rs = slice(c * CHUNK, (c + 1) * CHUNK)
        for p in range(N_PAIR):
            sl = slice(p * PAIR, (p + 1) * PAIR)
            units[c, p] = dict({name: t[rs, sl] for name, t in tile.items()}, d=direction)
    return units, jnp.exp(tot)


def _scan_tiles(scans, states):
    masks = [_time_masks(d) for d in range(len(scans))]
    units, decay = [], []
    for d, scan in enumerate(scans):
        u, dec = _scan_prepare(d, *scan)
        units.append(u)
        decay.append(dec)
    n_chunk = decay[0].shape[0] // CHUNK
    _chunk_affine([u for per_dir in units for u in per_dir.values()], masks)

    states = [list(s) for s in states]
    y_rows = [[None] * n_chunk for _ in scans]
    for j in range(n_chunk):
        for d in range(len(scans)):
            c = j if d == 0 else n_chunk - 1 - j
            ys = []
            for p in range(N_PAIR):
                u = units[d][c, p]
                s = states[d][p]
                s_bf = _bf(s)
                ys.append(_dot_nt(_bf(u["qhat"]), s_bf) + u["y0"])
                p_c = decay[d][c * CHUNK:c * CHUNK + 1, p * PAIR:(p + 1) * PAIR]
                states[d][p] = s * p_c + _dot(s_bf, _bf(u["mc"])) + u["c0"]
            y_rows[d][c] = jnp.concatenate(ys, axis=1)
    return [jnp.concatenate(rows, axis=0) for rows in y_rows], states


def _rwkv_pre(direction, z, edge, mu, w0, wup, a0, aup, gup, k_k, k_a, ones_bd):
    tt = z.shape[0]
    rows = lax.broadcasted_iota(jnp.int32, (tt, 1), 0)
    if direction == 0:
        prev = jnp.where(rows == 0, edge, pltpu.roll(z, 1, 0))
    else:
        prev = jnp.where(rows == tt - 1, edge, pltpu.roll(z, tt - 1, 0))
    zs = z + mu * (prev - z)
    r = zs[:, 0:D_RWKV]
    k = zs[:, D_RWKV:2 * D_RWKV]
    v = zs[:, 2 * D_RWKV:3 * D_RWKV]
    lora = zs[:, 3 * D_RWKV:3 * D_RWKV + LORA_WA]
    gl = zs[:, 3 * D_RWKV + LORA_WA:]

    w_pre = w0 + _dot(_bf(jnp.tanh(lora)), wup)
    le = _sigmoid(w_pre) * DECAY_SCALE
    a_sig = _sigmoid(a0 + _dot(_bf(lora), aup))
    g = _dot(_bf(_sigmoid(gl)), gup)
    kk = k * k_k
    kk = kk * lax.rsqrt(jnp.maximum(_seg_sum(kk * kk, ones_bd), 1e-24))
    k = k * (1.0 + (a_sig - 1.0) * k_a)
    return r, k, v, -kk, kk * a_sig, le, g


def _rwkv_post(y, r, k, v, g, r_k, ln_w, ln_b, ones_bd):
    mean = _seg_sum(y, ones_bd) * (1.0 / HEAD_DIM)
    yc = y - mean
    var = _seg_sum(yc * yc, ones_bd) * (1.0 / HEAD_DIM)
    yn = yc * lax.rsqrt(var + GN_EPS) * ln_w + ln_b
    bonus = _seg_sum(r * k * r_k, ones_bd) * v
    return (yn + bonus) * g


def _rwkv_kernel(zf_ref, hf_ref, zb_ref, hb_ref, mu_ref, w0_ref, wup_ref, a0_ref, aup_ref,
                 gup_ref, kk_ref, ka_ref, rk_ref, lnw_ref, lnb_ref, ones_ref, tmask_ref,
                 of_ref, ob_ref, s_ref):
    i = pl.program_id(1)

    @pl.when(i == 0)
    def _():
        s_ref[...] = jnp.zeros_like(s_ref)

    ones_bd = ones_ref[...]
    edges = (jnp.where(i > 0, hf_ref[0, HALO - 1:HALO, :], 0.0),
             jnp.where(i > 0, hb_ref[0, 0:1, :], 0.0))
    pre = [_rwkv_pre(d, z_ref[0], edges[d], mu_ref[d], w0_ref[d], wup_ref[d], a0_ref[d],
                     aup_ref[d], gup_ref[d], kk_ref[d], ka_ref[d], ones_bd)
           for d, z_ref in enumerate((zf_ref, zb_ref))]
    states = [[s_ref[d, p] for p in range(N_PAIR)] for d in range(2)]
    ys, states = _scan_tiles([pre[d][:6] + (tmask_ref[d],) for d in range(2)], states)
    for d, o_ref in enumerate((of_ref, ob_ref)):
        r, k, v, _, _, _, g = pre[d]
        o_ref[0] = _rwkv_post(ys[d], r, k, v, g, rk_ref[d], lnw_ref[d], lnb_ref[d], ones_bd)
        for p in range(N_PAIR):
            s_ref[d, p] = states[d][p]


def _chunk_sum_masks(tt):
    t = np.arange(tt)
    same = (t[:, None] // CHUNK) == (t[None, :] // CHUNK)
    fwd = same & (t[None, :] <= t[:, None])
    bwd = same & (t[None, :] >= t[:, None])
    return jnp.asarray(np.stack([fwd, bwd]), dtype=BF16)


def _rwkv(z_rwkv, mu, w0, wup, a0, aup, gup, k_k, k_a, r_k, ln_w, ln_b, ones_bd):
    bsz, seq, _ = z_rwkv.shape
    tt = TILE_SCAN
    n_tile = seq // tt
    blk = tt // HALO
    last_blk = n_tile * blk - 1

    def whole(arr):
        return pl.BlockSpec(arr.shape, lambda b, i: (0,) * arr.ndim)

    tmask = _chunk_sum_masks(tt)
    params = (mu, w0, wup, a0, aup, gup, k_k, k_a, r_k, ln_w, ln_b, ones_bd, tmask)
    tile = (1, tt, D_RWKV_IN)
    halo = (1, HALO, D_RWKV_IN)
    out_tile = (1, tt, D_RWKV)
    return pl.pallas_call(
        _rwkv_kernel,
        grid=(bsz, n_tile),
        in_specs=[pl.BlockSpec(tile, lambda b, i: (b, i, 0)),
                  pl.BlockSpec(halo, lambda b, i: (b, jnp.maximum(i * blk - 1, 0), 0)),
                  pl.BlockSpec(tile, lambda b, i: (b, n_tile - 1 - i, 0)),
                  pl.BlockSpec(halo, lambda b, i: (b, jnp.minimum((n_tile - i) * blk, last_blk), 0)),
                  ] + [whole(p) for p in params],
        out_specs=[pl.BlockSpec(out_tile, lambda b, i: (b, i, 0)),
                   pl.BlockSpec(out_tile, lambda b, i: (b, n_tile - 1 - i, 0))],
        out_shape=[jax.ShapeDtypeStruct((bsz, seq, D_RWKV), F32)] * 2,
        scratch_shapes=[pltpu.VMEM((2, N_PAIR, PAIR, PAIR), F32)],
        compiler_params=pltpu.CompilerParams(
            dimension_semantics=("arbitrary", "arbitrary"), vmem_limit_bytes=VMEM_LIMIT),
        name="rwkv7_scan",
    )(z_rwkv, z_rwkv, z_rwkv, z_rwkv, *params)


def _out_kernel(x_ref, zc_ref, zp_ref, zn_ref, of_ref, ob_ref, mod_ref, cw_ref, g_post_mix_ref,
                g_pre_ffn_ref, g_post_ffn_ref, wout_ref, w1_ref, w2_ref, y_ref):
    i = pl.program_id(1)
    n = pl.num_programs(1)
    tm = x_ref.shape[1]
    m = mod_ref[0]
    x = x_ref[0]

    zc = zc_ref[0]
    gate_b = zc[:, 0:D_CONV]
    cu = zc[:, D_CONV:2 * D_CONV] * zc[:, 2 * D_CONV:]
    zp = zp_ref[0]
    zn = zn_ref[0]
    cu_prev = zp[HALO - 1:HALO, D_CONV:2 * D_CONV] * zp[HALO - 1:HALO, 2 * D_CONV:]
    cu_next = zn[0:1, D_CONV:2 * D_CONV] * zn[0:1, 2 * D_CONV:]
    cu_prev = jnp.where(i > 0, cu_prev, 0.0)
    cu_next = jnp.where(i < n - 1, cu_next, 0.0)
    rows = lax.broadcasted_iota(jnp.int32, (tm, 1), 0)
    before = jnp.where(rows == 0, cu_prev, pltpu.roll(cu, 1, 0))
    after = jnp.where(rows == tm - 1, cu_next, pltpu.roll(cu, tm - 1, 0))
    cw = cw_ref[...]
    y_conv = gate_b * (cw[0:1] * before + cw[1:2] * cu + cw[2:3] * after)

    y_rwkv = of_ref[0] + ob_ref[0]
    mix = (_dot(y_conv.astype(BF16), wout_ref[:D_CONV, :])
           + _dot(y_rwkv.astype(BF16), wout_ref[D_CONV:, :]))
    x1 = x + m[2:3] * _rms(mix, g_post_mix_ref[...])

    h = _rms(x1, g_pre_ffn_ref[...]) * (1.0 + m[4:5]) + m[3:4]
    hb = h.astype(BF16)
    f = None
    for j in range(0, D_FF, FF_CHUNK):
        hid = jnp.square(jnp.maximum(_dot(hb, w1_ref[:, j:j + FF_CHUNK]), 0.0))
        part = _dot(hid.astype(BF16), w2_ref[j:j + FF_CHUNK, :])
        f = part if f is None else f + part
    y_ref[0] = x1 + m[5:6] * _rms(f, g_post_ffn_ref[...])


def _outproj_ffn(x, z_conv, o_fwd, o_bwd, mod, conv_w_pad, g_post_mix, g_pre_ffn, g_post_ffn,
                 w_out_bf, w1_bf, w2_bf):
    bsz, seq, _ = x.shape
    tm = TILE_OUT
    n_tile = seq // tm
    blk = tm // HALO

    def const(shape):
        return pl.BlockSpec(shape, lambda b, i: (0,) * len(shape),
                            pipeline_mode=pl.Buffered(1))

    return pl.pallas_call(
        _out_kernel,
        grid=(bsz, n_tile),
        in_specs=[pl.BlockSpec((1, tm, D_MODEL), lambda b, i: (b, i, 0)),
                  pl.BlockSpec((1, tm, D_CONV_IN), lambda b, i: (b, i, 0)),
                  pl.BlockSpec((1, HALO, D_CONV_IN),
                               lambda b, i: (b, jnp.maximum(i * blk - 1, 0), 0)),
                  pl.BlockSpec((1, HALO, D_CONV_IN),
                               lambda b, i: (b, jnp.minimum((i + 1) * blk, n_tile * blk - 1), 0)),
                  pl.BlockSpec((1, tm, D_RWKV), lambda b, i: (b, i, 0)),
                  pl.BlockSpec((1, tm, D_RWKV), lambda b, i: (b, i, 0)),
                  pl.BlockSpec((1, 6, D_MODEL), lambda b, i: (b, 0, 0)),
                  const((HALO, D_CONV)), const((1, D_MODEL)), const((1, D_MODEL)),
                  const((1, D_MODEL)), const((D_MODEL, D_MODEL)), const((D_MODEL, D_FF)),
                  const((D_FF, D_MODEL))],
        out_specs=pl.BlockSpec((1, tm, D_MODEL), lambda b, i: (b, i, 0)),
        out_shape=jax.ShapeDtypeStruct((bsz, seq, D_MODEL), F32),
        compiler_params=pltpu.CompilerParams(
            dimension_semantics=("arbitrary", "arbitrary"), vmem_limit_bytes=VMEM_LIMIT),
        name="out_proj_ffn",
    )(x, z_conv, z_conv, z_conv, o_fwd, o_bwd, mod, conv_w_pad, g_post_mix, g_pre_ffn,
      g_post_ffn, w_out_bf, w1_bf, w2_bf)


def _lora_weight(w, top):
    return _bf(jnp.pad(w, ((0, 0), (top, LORA_WA - top - w.shape[1]), (0, 0))))


def _layer(xs, cs, w_ada, b_ada, g_pre_mix, g_post_mix, w_in, conv_w, mu_shift, w0, w_up, a0,
           a_up, g_up, k_k, k_a, r_k, ln_x_w, ln_x_b, w_out, g_pre_ffn, g_post_ffn, w_ffn1,
           w_ffn2):
    n_seq = sum(c.shape[0] for c in cs)
    c_all = jnp.pad(jnp.concatenate(cs, axis=0), ((0, HALO - n_seq), (0, 0)))
    mod = _modulation(c_all, w_ada, b_ada[None, :]).reshape(HALO, 6, D_MODEL)

    w_in_bf = w_in.astype(BF16)
    w_out_bf = w_out.astype(BF16)
    w1_bf = w_ffn1.astype(BF16)
    w2_bf = w_ffn2.astype(BF16)
    row = lambda p: p[None, :]
    dir_row = lambda p: p.reshape(2, 1, -1)
    wup = _lora_weight(w_up, 0)
    aup = _lora_weight(a_up, LORA_WA // 2)
    gup = _lora_weight(g_up, 0)
    conv_w_pad = jnp.pad(conv_w, ((0, HALO - conv_w.shape[0]), (0, 0)))
    head = jnp.arange(SEG_LANES) // HEAD_DIM
    ones_bd = (head[:, None] == head[None, :]).astype(BF16)

    outs = []
    start = 0
    for x in xs:
        bsz = x.shape[0]
        m = mod[start:start + bsz]
        start += bsz
        z_conv, z_rwkv = _inproj(x, m, row(g_pre_mix), w_in_bf)
        o_fwd, o_bwd = _rwkv(z_rwkv, dir_row(mu_shift), dir_row(w0), wup, dir_row(a0), aup, gup,
                             dir_row(k_k), dir_row(k_a), dir_row(r_k), dir_row(ln_x_w),
                             dir_row(ln_x_b), ones_bd)
        outs.append(_outproj_ffn(x, z_conv, o_fwd, o_bwd, m, conv_w_pad, row(g_post_mix),
                                 row(g_pre_ffn), row(g_post_ffn), w_out_bf, w1_bf, w2_bf))
    return outs


def kernel(x_prompt, x_sample, c_prompt, c_sample, w_ada, b_ada, g_pre_mix, g_post_mix, w_in,
           conv_w, mu_shift, w0, w_up, a0, a_up, g_up, k_k, k_a, r_k, ln_x_w, ln_x_b, w_out,
           g_pre_ffn, g_post_ffn, w_ffn1, w_ffn2):
    xs = [x_prompt, x_sample]
    for l in range(w_ada.shape[0]):
        xs = _layer(xs, [c_prompt, c_sample], w_ada[l], b_ada[l], g_pre_mix[l], g_post_mix[l],
                    w_in[l], conv_w[l], mu_shift[l], w0[l], w_up[l], a0[l], a_up[l], g_up[l],
                    k_k[l], k_a[l], r_k[l], ln_x_w[l], ln_x_b[l], w_out[l], g_pre_ffn[l],
                    g_post_ffn[l], w_ffn1[l], w_ffn2[l])
    return (xs[0], xs[1])
```

```python
import math

import numpy as np

import jax
import jax.numpy as jnp
from jax import lax
from jax.experimental import pallas as pl
from jax.experimental.pallas import tpu as pltpu

D_MODEL = 1024
D_CONV = 512
D_RWKV = 512
HEAD_DIM = 64
LORA_WA = 128
GATE_RANK = 128
D_RWKV_IN = 3 * D_RWKV + LORA_WA + GATE_RANK
D_CONV_IN = 3 * D_CONV
D_FF = 4096
NORM_EPS = 1e-6
GN_EPS = 64e-5
LOG2_DECAY_SCALE = -math.exp(-0.5) * math.log2(math.e)

CHUNK = 64
PAIR = 2 * HEAD_DIM
N_PAIR = D_RWKV // PAIR
HALO = 8
SEG_LANES = 256

TILE_IN = 512
TILE_SCAN = 256
TILE_OUT = 512
ROW_BLOCK = 256
FF_CHUNK = 1024
VMEM_LIMIT = 56 * 1024 * 1024

F32 = jnp.float32
BF16 = jnp.bfloat16
HI = lax.Precision.HIGHEST


def _dot(a, b, precision=None):
    return jnp.dot(a, b, precision=precision, preferred_element_type=F32)


def _dot_nt(a, b):
    return lax.dot_general(a, b, (((1,), (1,)), ((), ())), preferred_element_type=F32)


def _dot_tn(a, b):
    return lax.dot_general(a, b, (((0,), (0,)), ((), ())), preferred_element_type=F32)


def _bf(x):
    return x.astype(BF16)


def _split2(x):
    hi = x.astype(BF16)
    return hi, (x - hi.astype(F32)).astype(BF16)


def _sigmoid(x):
    return 1.0 / (1.0 + jnp.exp(-x))


def _rms(x, g):
    return x * lax.rsqrt(jnp.mean(x * x, axis=-1, keepdims=True) + NORM_EPS) * g


def _seg_sum(x, ones_bd):
    xb = _bf(x)
    return jnp.concatenate([_dot(xb[:, j:j + SEG_LANES], ones_bd)
                            for j in range(0, x.shape[1], SEG_LANES)], axis=1)


def _row_blocks(rows):
    return [slice(j, j + ROW_BLOCK) for j in range(0, rows, ROW_BLOCK)]


def _mod_kernel(c_ref, w_ref, b_ref, o_ref):
    c = c_ref[...]
    s = c * _sigmoid(c)
    o_ref[...] = _dot(s, w_ref[...], HI) + b_ref[...]


def _modulation(c_all, w_ada, b_ada):
    rows = c_all.shape[0]
    n_blk = w_ada.shape[1] // D_MODEL
    return pl.pallas_call(
        _mod_kernel,
        grid=(n_blk,),
        in_specs=[pl.BlockSpec((rows, D_MODEL), lambda j: (0, 0)),
                  pl.BlockSpec((D_MODEL, D_MODEL), lambda j: (0, j)),
                  pl.BlockSpec((1, D_MODEL), lambda j: (0, j))],
        out_specs=pl.BlockSpec((rows, D_MODEL), lambda j: (0, j)),
        out_shape=jax.ShapeDtypeStruct((rows, w_ada.shape[1]), F32),
        name="adaln_mod",
    )(c_all, w_ada, b_ada)


def _inproj_kernel(x_ref, mod_ref, g_ref, w_ref, zc_ref, zr_ref):
    m = mod_ref[0]
    for rows in _row_blocks(x_ref.shape[1]):
        h = _rms(x_ref[0, rows, :], g_ref[...]) * (1.0 + m[1:2]) + m[0:1]
        hb = h.astype(BF16)
        zc_ref[0, rows, :] = _dot(hb, w_ref[:, :D_CONV_IN])
        zr_ref[0, rows, :] = _dot(hb, w_ref[:, D_CONV_IN:])


def _inproj(x, mod, g_pre, w_in_bf):
    bsz, seq, _ = x.shape
    tm = TILE_IN
    d_in = w_in_bf.shape[1]
    return pl.pallas_call(
        _inproj_kernel,
        grid=(bsz, seq // tm),
        in_specs=[pl.BlockSpec((1, tm, D_MODEL), lambda b, i: (b, i, 0)),
                  pl.BlockSpec((1, 6, D_MODEL), lambda b, i: (b, 0, 0)),
                  pl.BlockSpec((1, D_MODEL), lambda b, i: (0, 0)),
                  pl.BlockSpec((D_MODEL, d_in), lambda b, i: (0, 0),
                               pipeline_mode=pl.Buffered(1))],
        out_specs=[pl.BlockSpec((1, tm, D_CONV_IN), lambda b, i: (b, i, 0)),
                   pl.BlockSpec((1, tm, D_RWKV_IN), lambda b, i: (b, i, 0))],
        out_shape=[jax.ShapeDtypeStruct((bsz, seq, D_CONV_IN), F32),
                   jax.ShapeDtypeStruct((bsz, seq, D_RWKV_IN), F32)],
        compiler_params=pltpu.CompilerParams(
            dimension_semantics=("arbitrary", "arbitrary"), vmem_limit_bytes=VMEM_LIMIT),
        name="in_proj",
    )(x, mod, g_pre, w_in_bf)


def _time_masks(direction):
    sign = 1 if direction == 0 else -1
    row = lax.broadcasted_iota(jnp.int32, (CHUNK, PAIR), 0)
    lane = lax.broadcasted_iota(jnp.int32, (CHUNK, PAIR), 1)
    delta = (row - jnp.bitwise_and(lane, HEAD_DIM - 1)) * sign
    q_row = lax.broadcasted_iota(jnp.int32, (PAIR, PAIR), 0)
    q_lane = lax.broadcasted_iota(jnp.int32, (PAIR, PAIR), 1)
    return dict(strict=delta > 0, incl=delta >= 0,
                eye=jnp.where(delta == 0, 1.0, 0.0).astype(F32),
                lane_lo=lane < HEAD_DIM,
                bd=(q_row < HEAD_DIM) == (q_lane < HEAD_DIM))


def _block_diag(y, lane_lo):
    return _bf(jnp.concatenate([jnp.where(lane_lo, y, 0.0), jnp.where(lane_lo, 0.0, y)], axis=0))


def _chunk_affine(units, masks):
    lo = masks[0]["lane_lo"]
    bd = masks[0]["bd"]
    for u in units:
        u["aa"] = _dot_nt(_bf(jnp.concatenate([u["at"], u["rt"]], axis=0)),
                          jnp.concatenate([_block_diag(u["bt"], lo), _block_diag(u["kt"], lo)],
                                          axis=0))
    for u in units:
        m = masks[u["d"]]
        aa = u.pop("aa")
        u["n_ab"] = jnp.where(m["strict"], aa[:CHUNK, :PAIR], 0.0)
        u["a_ak"] = jnp.where(m["strict"], aa[:CHUNK, PAIR:], 0.0)
        u["a_rb"] = jnp.where(m["incl"], aa[CHUNK:, :PAIR], 0.0)
        u["a_rk"] = jnp.where(m["incl"], aa[CHUNK:, PAIR:], 0.0)
    for u in units:
        u["x"] = masks[u["d"]]["eye"] + u["n_ab"]
        u["pw"] = _dot(_bf(u["n_ab"]), _block_diag(u["n_ab"], lo))
    for u in units:
        av = _dot(_bf(jnp.concatenate([u["a_ak"], u["a_rk"]], axis=0)), _block_diag(u["v"], lo))
        u["akv"], u["rkv"] = av[:CHUNK], av[CHUNK:]
    width = 2
    while 2 * width < CHUNK:
        for u in units:
            zz = _dot(_bf(u["pw"]), jnp.concatenate([_block_diag(u["pw"], lo),
                                                     _block_diag(u["x"], lo)], axis=1))
            u["pw"] = zz[:, :PAIR]
            u["x"] = u["x"] + zz[:, PAIR:]
        width *= 2
    for u in units:
        u["x"] = u["x"] + _dot(_bf(u["pw"]), _block_diag(u["x"], lo))
    for u in units:
        wv = _dot(_bf(u["x"]), jnp.concatenate([_block_diag(u["at_true"], lo),
                                                _block_diag(u["akv"], lo)], axis=1))
        u["wt"], u["vt"] = wv[:, :PAIR], wv[:, PAIR:]
    for u in units:
        qy = _dot(_bf(u["a_rb"]), jnp.concatenate([_block_diag(u["wt"], lo),
                                                   _block_diag(u["vt"], lo)], axis=1))
        u["qhat"] = u["rt_true"] + qy[:, :PAIR]
        u["y0"] = qy[:, PAIR:] + u["rkv"]
    for u in units:
        u["mc"] = jnp.where(bd, _dot_tn(_bf(u["wt"]), _bf(u["bh"])), 0.0)
    for u in units:
        u["c0"] = jnp.where(bd, _dot_tn(_bf(jnp.concatenate([u["vt"], u["v"]], axis=0)),
                                        _bf(jnp.concatenate([u["bh"], u["kh"]], axis=0))), 0.0)


def _scan_prepare(direction, r, k, v, a, b, le, tmask):
    tt = r.shape[0]
    le_hi, le_lo = _split2(le)
    cum = _dot(tmask, le_hi) + _dot(tmask, le_lo)
    last = CHUNK - 1 if direction == 0 else 0
    tot = jnp.concatenate(
        [jnp.broadcast_to(cum[c * CHUNK + last:c * CHUNK + last + 1], (CHUNK, D_RWKV))
         for c in range(tt // CHUNK)], axis=0)
    off = 0.5 * tot
    ci = cum - off
    e_off = jnp.exp2(off)
    e_neg = jnp.exp2(-ci)
    tile = dict(at=a * jnp.exp2(ci - le), rt=r * jnp.exp2(ci), bt=b * e_neg, kt=k * e_neg, v=v)
    tile["bh"] = tile["bt"] * e_off
    tile["kh"] = tile["kt"] * e_off
    tile["at_true"] = tile["at"] * e_off
    tile["rt_true"] = tile["rt"] * e_off
    units = {}
    for c in range(tt // CHUNK):
        rs = slice(c * CHUNK, (c + 1) * CHUNK)
        for p in range(N_PAIR):
            sl = slice(p * PAIR, (p + 1) * PAIR)
            units[c, p] = dict({name: t[rs, sl] for name, t in tile.items()}, d=direction)
    return units, jnp.exp2(tot)


def _scan_tiles(scans, states):
    masks = [_time_masks(d) for d in range(len(scans))]
    units, decay = [], []
    for d, scan in enumerate(scans):
        u, dec = _scan_prepare(d, *scan)
        units.append(u)
        decay.append(dec)
    n_chunk = decay[0].shape[0] // CHUNK
    _chunk_affine([u for per_dir in units for u in per_dir.values()], masks)

    states = [list(s) for s in states]
    y_rows = [[None] * n_chunk for _ in scans]
    for j in range(n_chunk):
        for d in range(len(scans)):
            c = j if d == 0 else n_chunk - 1 - j
            ys = []
            for p in range(N_PAIR):
                u = units[d][c, p]
                s = states[d][p]
                s_bf = _bf(s)
                ys.append(_dot_nt(_bf(u["qhat"]), s_bf) + u["y0"])
                p_c = decay[d][c * CHUNK:c * CHUNK + 1, p * PAIR:(p + 1) * PAIR]
                states[d][p] = s * p_c + _dot(s_bf, _bf(u["mc"])) + u["c0"]
            y_rows[d][c] = jnp.concatenate(ys, axis=1)
    return [jnp.concatenate(rows, axis=0) for rows in y_rows], states


def _rwkv_pre(direction, z, edge, mu, w0, wup, a0, aup, gup, k_k, k_a, ones_bd):
    tt = z.shape[0]
    row_id = lax.broadcasted_iota(jnp.int32, (HALO, 1), 0)
    if direction == 0:
        prev = pltpu.roll(z, 1, 0)
        prev = jnp.concatenate([jnp.where(row_id == 0, edge, prev[:HALO]), prev[HALO:]], axis=0)
    else:
        prev = pltpu.roll(z, tt - 1, 0)
        prev = jnp.concatenate([prev[:tt - HALO],
                                jnp.where(row_id == HALO - 1, edge, prev[tt - HALO:])], axis=0)
    zs = z + mu * (prev - z)
    r = zs[:, 0:D_RWKV]
    k = zs[:, D_RWKV:2 * D_RWKV]
    v = zs[:, 2 * D_RWKV:3 * D_RWKV]
    lora = zs[:, 3 * D_RWKV:3 * D_RWKV + LORA_WA]
    gl = zs[:, 3 * D_RWKV + LORA_WA:]

    w_pre = w0 + _dot(_bf(jnp.tanh(lora)), wup)
    le = _sigmoid(w_pre) * LOG2_DECAY_SCALE
    a_sig = _sigmoid(a0 + _dot(_bf(lora), aup))
    g = _dot(_bf(_sigmoid(gl)), gup)
    kk = k * k_k
    kk = kk * lax.rsqrt(jnp.maximum(_seg_sum(kk * kk, ones_bd), 1e-24))
    k = k * (1.0 + (a_sig - 1.0) * k_a)
    return r, k, v, -kk, kk * a_sig, le, g


def _rwkv_post(y, r, k, v, g, r_k, ln_w, ln_b, ones_bd):
    mean = _seg_sum(y, ones_bd) * (1.0 / HEAD_DIM)
    yc = y - mean
    var = _seg_sum(yc * yc, ones_bd) * (1.0 / HEAD_DIM)
    yn = yc * lax.rsqrt(var + GN_EPS) * ln_w + ln_b
    bonus = _seg_sum(r * k * r_k, ones_bd) * v
    return (yn + bonus) * g


def _rwkv_kernel(zf_ref, hf_ref, zb_ref, hb_ref, mu_ref, w0_ref, wup_ref, a0_ref, aup_ref,
                 gup_ref, kk_ref, ka_ref, rk_ref, lnw_ref, lnb_ref, ones_ref, tmask_ref,
                 of_ref, ob_ref, s_ref):
    i = pl.program_id(1)

    @pl.when(i == 0)
    def _():
        s_ref[...] = jnp.zeros_like(s_ref)

    ones_bd = ones_ref[...]
    edges = (jnp.where(i > 0, hf_ref[0, HALO - 1:HALO, :], 0.0),
             jnp.where(i > 0, hb_ref[0, 0:1, :], 0.0))
    pre = [_rwkv_pre(d, z_ref[0], edges[d], mu_ref[d], w0_ref[d], wup_ref[d], a0_ref[d],
                     aup_ref[d], gup_ref[d], kk_ref[d], ka_ref[d], ones_bd)
           for d, z_ref in enumerate((zf_ref, zb_ref))]
    states = [[s_ref[d, p] for p in range(N_PAIR)] for d in range(2)]
    ys, states = _scan_tiles([pre[d][:6] + (tmask_ref[d],) for d in range(2)], states)
    for d, o_ref in enumerate((of_ref, ob_ref)):
        r, k, v, _, _, _, g = pre[d]
        o_ref[0] = _rwkv_post(ys[d], r, k, v, g, rk_ref[d], lnw_ref[d], lnb_ref[d], ones_bd)
        for p in range(N_PAIR):
            s_ref[d, p] = states[d][p]


def _chunk_sum_masks(tt):
    t = np.arange(tt)
    same = (t[:, None] // CHUNK) == (t[None, :] // CHUNK)
    fwd = same & (t[None, :] <= t[:, None])
    bwd = same & (t[None, :] >= t[:, None])
    return jnp.asarray(np.stack([fwd, bwd]), dtype=BF16)


def _rwkv(z_rwkv, mu, w0, wup, a0, aup, gup, k_k, k_a, r_k, ln_w, ln_b, ones_bd):
    bsz, seq, _ = z_rwkv.shape
    tt = TILE_SCAN
    n_tile = seq // tt
    blk = tt // HALO
    last_blk = n_tile * blk - 1

    def whole(arr):
        return pl.BlockSpec(arr.shape, lambda b, i: (0,) * arr.ndim)

    tmask = _chunk_sum_masks(tt)
    params = (mu, w0, wup, a0, aup, gup, k_k, k_a, r_k, ln_w, ln_b, ones_bd, tmask)
    tile = (1, tt, D_RWKV_IN)
    halo = (1, HALO, D_RWKV_IN)
    out_tile = (1, tt, D_RWKV)
    return pl.pallas_call(
        _rwkv_kernel,
        grid=(bsz, n_tile),
        in_specs=[pl.BlockSpec(tile, lambda b, i: (b, i, 0)),
                  pl.BlockSpec(halo, lambda b, i: (b, jnp.maximum(i * blk - 1, 0), 0)),
                  pl.BlockSpec(tile, lambda b, i: (b, n_tile - 1 - i, 0)),
                  pl.BlockSpec(halo, lambda b, i: (b, jnp.minimum((n_tile - i) * blk, last_blk), 0)),
                  ] + [whole(p) for p in params],
        out_specs=[pl.BlockSpec(out_tile, lambda b, i: (b, i, 0)),
                   pl.BlockSpec(out_tile, lambda b, i: (b, n_tile - 1 - i, 0))],
        out_shape=[jax.ShapeDtypeStruct((bsz, seq, D_RWKV), F32)] * 2,
        scratch_shapes=[pltpu.VMEM((2, N_PAIR, PAIR, PAIR), F32)],
        compiler_params=pltpu.CompilerParams(
            dimension_semantics=("arbitrary", "arbitrary"), vmem_limit_bytes=VMEM_LIMIT),
        name="rwkv7_scan",
    )(z_rwkv, z_rwkv, z_rwkv, z_rwkv, *params)


def _out_kernel(x_ref, zc_ref, zp_ref, zn_ref, of_ref, ob_ref, mod_ref, cw_ref, g_post_mix_ref,
                g_pre_ffn_ref, g_post_ffn_ref, wout_ref, w1_ref, w2_ref, y_ref):
    i = pl.program_id(1)
    n = pl.num_programs(1)
    tm = x_ref.shape[1]
    m = mod_ref[0]
    cw = cw_ref[...]
    blocks = _row_blocks(tm)
    row_id = lax.broadcasted_iota(jnp.int32, (HALO, 1), 0)

    def gated(zrows):
        return zrows[:, D_CONV:2 * D_CONV] * zrows[:, 2 * D_CONV:]

    def mixer_inputs(rows):
        zc = zc_ref[0, rows, :]
        cu = gated(zc)
        if rows.start == 0:
            cu_prev = jnp.where(i > 0, gated(zp_ref[0, HALO - 1:HALO, :]), 0.0)
        else:
            cu_prev = gated(zc_ref[0, rows.start - 1:rows.start, :])
        if rows.stop == tm:
            cu_next = jnp.where(i < n - 1, gated(zn_ref[0, 0:1, :]), 0.0)
        else:
            cu_next = gated(zc_ref[0, rows.stop:rows.stop + 1, :])
        before = pltpu.roll(cu, 1, 0)
        before = jnp.concatenate([jnp.where(row_id == 0, cu_prev, before[:HALO]), before[HALO:]],
                                 axis=0)
        after = pltpu.roll(cu, ROW_BLOCK - 1, 0)
        after = jnp.concatenate([after[:ROW_BLOCK - HALO],
                                 jnp.where(row_id == HALO - 1, cu_next, after[ROW_BLOCK - HALO:])],
                                axis=0)
        y_conv = zc[:, 0:D_CONV] * (cw[0:1] * before + cw[1:2] * cu + cw[2:3] * after)
        y_rwkv = of_ref[0, rows, :] + ob_ref[0, rows, :]
        return y_conv.astype(BF16), y_rwkv.astype(BF16)

    def out_proj(inputs):
        y_conv, y_rwkv = inputs
        return _dot(y_conv, wout_ref[:D_CONV, :]) + _dot(y_rwkv, wout_ref[D_CONV:, :])

    def mlp_input(rows, mix):
        x1 = x_ref[0, rows, :] + m[2:3] * _rms(mix, g_post_mix_ref[...])
        h = _rms(x1, g_pre_ffn_ref[...]) * (1.0 + m[4:5]) + m[3:4]
        return x1, h.astype(BF16)

    def mlp(hb):
        f = None
        for j in range(0, D_FF, FF_CHUNK):
            hid = jnp.square(jnp.maximum(_dot(hb, w1_ref[:, j:j + FF_CHUNK]), 0.0))
            part = _dot(hid.astype(BF16), w2_ref[j:j + FF_CHUNK, :])
            f = part if f is None else f + part
        return f

    mixes = []
    for rows in blocks:
        mixes.append(out_proj(mixer_inputs(rows)))
    mid = [mlp_input(rows, mix) for rows, mix in zip(blocks, mixes)]
    for rows, (x1, hb) in zip(blocks, mid):
        y_ref[0, rows, :] = x1 + m[5:6] * _rms(mlp(hb), g_post_ffn_ref[...])


def _outproj_ffn(x, z_conv, o_fwd, o_bwd, mod, conv_w_pad, g_post_mix, g_pre_ffn, g_post_ffn,
                 w_out_bf, w1_bf, w2_bf):
    bsz, seq, _ = x.shape
    tm = TILE_OUT
    n_tile = seq // tm
    blk = tm // HALO

    def const(shape):
        return pl.BlockSpec(shape, lambda b, i: (0,) * len(shape),
                            pipeline_mode=pl.Buffered(1))

    return pl.pallas_call(
        _out_kernel,
        grid=(bsz, n_tile),
        in_specs=[pl.BlockSpec((1, tm, D_MODEL), lambda b, i: (b, i, 0)),
                  pl.BlockSpec((1, tm, D_CONV_IN), lambda b, i: (b, i, 0)),
                  pl.BlockSpec((1, HALO, D_CONV_IN),
                               lambda b, i: (b, jnp.maximum(i * blk - 1, 0), 0)),
                  pl.BlockSpec((1, HALO, D_CONV_IN),
                               lambda b, i: (b, jnp.minimum((i + 1) * blk, n_tile * blk - 1), 0)),
                  pl.BlockSpec((1, tm, D_RWKV), lambda b, i: (b, i, 0)),
                  pl.BlockSpec((1, tm, D_RWKV), lambda b, i: (b, i, 0)),
                  pl.BlockSpec((1, 6, D_MODEL), lambda b, i: (b, 0, 0)),
                  const((HALO, D_CONV)), const((1, D_MODEL)), const((1, D_MODEL)),
                  const((1, D_MODEL)), const((D_MODEL, D_MODEL)), const((D_MODEL, D_FF)),
                  const((D_FF, D_MODEL))],
        out_specs=pl.BlockSpec((1, tm, D_MODEL), lambda b, i: (b, i, 0)),
        out_shape=jax.ShapeDtypeStruct((bsz, seq, D_MODEL), F32),
        compiler_params=pltpu.CompilerParams(
            dimension_semantics=("arbitrary", "arbitrary"), vmem_limit_bytes=VMEM_LIMIT),
        name="out_proj_ffn",
    )(x, z_conv, z_conv, z_conv, o_fwd, o_bwd, mod, conv_w_pad, g_post_mix, g_pre_ffn,
      g_post_ffn, w_out_bf, w1_bf, w2_bf)


def _lora_weight(w, top):
    return _bf(jnp.pad(w, ((0, 0), (top, LORA_WA - top - w.shape[1]), (0, 0))))


def _layer(xs, cs, w_ada, b_ada, g_pre_mix, g_post_mix, w_in, conv_w, mu_shift, w0, w_up, a0,
           a_up, g_up, k_k, k_a, r_k, ln_x_w, ln_x_b, w_out, g_pre_ffn, g_post_ffn, w_ffn1,
           w_ffn2):
    n_seq = sum(c.shape[0] for c in cs)
    c_all = jnp.pad(jnp.concatenate(cs, axis=0), ((0, HALO - n_seq), (0, 0)))
    mod = _modulation(c_all, w_ada, b_ada[None, :]).reshape(HALO, 6, D_MODEL)

    w_in_bf = w_in.astype(BF16)
    w_out_bf = w_out.astype(BF16)
    w1_bf = w_ffn1.astype(BF16)
    w2_bf = w_ffn2.astype(BF16)
    row = lambda p: p[None, :]
    dir_row = lambda p: p.reshape(2, 1, -1)
    wup = _lora_weight(w_up, 0)
    aup = _lora_weight(a_up, LORA_WA // 2)
    gup = _lora_weight(g_up, 0)
    conv_w_pad = jnp.pad(conv_w, ((0, HALO - conv_w.shape[0]), (0, 0)))
    head = jnp.arange(SEG_LANES) // HEAD_DIM
    ones_bd = (head[:, None] == head[None, :]).astype(BF16)

    outs = []
    start = 0
    for x in xs:
        bsz = x.shape[0]
        m = mod[start:start + bsz]
        start += bsz
        z_conv, z_rwkv = _inproj(x, m, row(g_pre_mix), w_in_bf)
        o_fwd, o_bwd = _rwkv(z_rwkv, dir_row(mu_shift), dir_row(w0), wup, dir_row(a0), aup, gup,
                             dir_row(k_k), dir_row(k_a), dir_row(r_k), dir_row(ln_x_w),
                             dir_row(ln_x_b), ones_bd)
        outs.append(_outproj_ffn(x, z_conv, o_fwd, o_bwd, m, conv_w_pad, row(g_post_mix),
                                 row(g_pre_ffn), row(g_post_ffn), w_out_bf, w1_bf, w2_bf))
    return outs


def kernel(x_prompt, x_sample, c_prompt, c_sample, w_ada, b_ada, g_pre_mix, g_post_mix, w_in,
           conv_w, mu_shift, w0, w_up, a0, a_up, g_up, k_k, k_a, r_k, ln_x_w, ln_x_b, w_out,
           g_pre_ffn, g_post_ffn, w_ffn1, w_ffn2):
    xs = [x_prompt, x_sample]
    for l in range(w_ada.shape[0]):
        xs = _layer(xs, [c_prompt, c_sample], w_ada[l], b_ada[l], g_pre_mix[l], g_post_mix[l],
                    w_in[l], conv_w[l], mu_shift[l], w0[l], w_up[l], a0[l], a_up[l], g_up[l],
                    k_k[l], k_a[l], r_k[l], ln_x_w[l], ln_x_b[l], w_out[l], g_pre_ffn[l],
                    g_post_ffn[l], w_ffn1[l], w_ffn2[l])
    return (xs[0], xs[1])
```

```python
import math

import numpy as np

import jax
import jax.numpy as jnp
from jax import lax
from jax.experimental import pallas as pl
from jax.experimental.pallas import tpu as pltpu

D_MODEL = 1024
D_CONV = 512
D_RWKV = 512
HEAD_DIM = 64
LORA_WA = 128
GATE_RANK = 128
D_RWKV_IN = 3 * D_RWKV + LORA_WA + GATE_RANK
D_CONV_IN = 3 * D_CONV
D_FF = 4096
NORM_EPS = 1e-6
GN_EPS = 64e-5
LOG2_DECAY_SCALE = -math.exp(-0.5) * math.log2(math.e)

CHUNK = 64
PAIR = 2 * HEAD_DIM
N_PAIR = D_RWKV // PAIR
HALO = 8
SEG_LANES = 256

TILE_IN = 1024
TILE_SCAN = 256
TILE_OUT = 512
ROW_BLOCK = 256
FF_CHUNK = 1024
CARRY_AFTER = (3, 8)
VMEM_LIMIT = 56 * 1024 * 1024

F32 = jnp.float32
BF16 = jnp.bfloat16
HI = lax.Precision.HIGHEST


def _dot(a, b, precision=None):
    return jnp.dot(a, b, precision=precision, preferred_element_type=F32)


def _dot_nt(a, b):
    return lax.dot_general(a, b, (((1,), (1,)), ((), ())), preferred_element_type=F32)


def _dot_tn(a, b):
    return lax.dot_general(a, b, (((0,), (0,)), ((), ())), preferred_element_type=F32)


def _bf(x):
    return x.astype(BF16)


def _split2(x):
    hi = x.astype(BF16)
    return hi, (x - hi.astype(F32)).astype(BF16)


def _sigmoid(x):
    return 1.0 / (1.0 + jnp.exp(-x))


def _rms(x, g):
    return x * lax.rsqrt(jnp.mean(x * x, axis=-1, keepdims=True) + NORM_EPS) * g


def _seg_sum(x, ones_bd):
    xb = _bf(x)
    return jnp.concatenate([_dot(xb[:, j:j + SEG_LANES], ones_bd)
                            for j in range(0, x.shape[1], SEG_LANES)], axis=1)


def _row_blocks(rows):
    return [slice(j, j + ROW_BLOCK) for j in range(0, rows, ROW_BLOCK)]


def _mod_kernel(c_ref, w_ref, b_ref, o_ref):
    c = c_ref[...]
    s = c * _sigmoid(c)
    o_ref[...] = _dot(s, w_ref[...], HI) + b_ref[...]


def _modulation(c_all, w_ada, b_ada):
    rows = c_all.shape[0]
    n_blk = w_ada.shape[1] // D_MODEL
    return pl.pallas_call(
        _mod_kernel,
        grid=(n_blk,),
        in_specs=[pl.BlockSpec((rows, D_MODEL), lambda j: (0, 0)),
                  pl.BlockSpec((D_MODEL, D_MODEL), lambda j: (0, j)),
                  pl.BlockSpec((1, D_MODEL), lambda j: (0, j))],
        out_specs=pl.BlockSpec((rows, D_MODEL), lambda j: (0, j)),
        out_shape=jax.ShapeDtypeStruct((rows, w_ada.shape[1]), F32),
        name="adaln_mod",
    )(c_all, w_ada, b_ada)


def _inproj_kernel(x_ref, mod_ref, g_ref, w_ref, zc_ref, zr_ref):
    m = mod_ref[0]
    for rows in _row_blocks(x_ref.shape[1]):
        h = _rms(x_ref[0, rows, :], g_ref[...]) * (1.0 + m[1:2]) + m[0:1]
        hb = h.astype(BF16)
        zc_ref[0, rows, :] = _dot(hb, w_ref[:, :D_CONV_IN])
        zr_ref[0, rows, :] = _dot(hb, w_ref[:, D_CONV_IN:])


def _inproj(x, mod, g_pre, w_in_bf):
    bsz, seq, _ = x.shape
    tm = TILE_IN
    d_in = w_in_bf.shape[1]
    return pl.pallas_call(
        _inproj_kernel,
        grid=(bsz, seq // tm),
        in_specs=[pl.BlockSpec((1, tm, D_MODEL), lambda b, i: (b, i, 0)),
                  pl.BlockSpec((1, 6, D_MODEL), lambda b, i: (b, 0, 0)),
                  pl.BlockSpec((1, D_MODEL), lambda b, i: (0, 0)),
                  pl.BlockSpec((D_MODEL, d_in), lambda b, i: (0, 0),
                               pipeline_mode=pl.Buffered(1))],
        out_specs=[pl.BlockSpec((1, tm, D_CONV_IN), lambda b, i: (b, i, 0)),
                   pl.BlockSpec((1, tm, D_RWKV_IN), lambda b, i: (b, i, 0))],
        out_shape=[jax.ShapeDtypeStruct((bsz, seq, D_CONV_IN), F32),
                   jax.ShapeDtypeStruct((bsz, seq, D_RWKV_IN), F32)],
        compiler_params=pltpu.CompilerParams(
            dimension_semantics=("arbitrary", "arbitrary"), vmem_limit_bytes=VMEM_LIMIT),
        name="in_proj",
    )(x, mod, g_pre, w_in_bf)


def _time_masks(direction):
    sign = 1 if direction == 0 else -1
    row = lax.broadcasted_iota(jnp.int32, (CHUNK, PAIR), 0)
    lane = lax.broadcasted_iota(jnp.int32, (CHUNK, PAIR), 1)
    delta = (row - jnp.bitwise_and(lane, HEAD_DIM - 1)) * sign
    q_row = lax.broadcasted_iota(jnp.int32, (PAIR, PAIR), 0)
    q_lane = lax.broadcasted_iota(jnp.int32, (PAIR, PAIR), 1)
    return dict(strict=delta > 0, incl=delta >= 0,
                eye=jnp.where(delta == 0, 1.0, 0.0).astype(F32),
                lane_lo=lane < HEAD_DIM,
                bd=(q_row < HEAD_DIM) == (q_lane < HEAD_DIM))


def _block_diag(y, lane_lo):
    return _bf(jnp.concatenate([jnp.where(lane_lo, y, 0.0), jnp.where(lane_lo, 0.0, y)], axis=0))


def _weave(main, *riders):
    n = 0
    for _ in main:
        n += 1
        for stages, positions in riders:
            for _ in range(positions.count(n)):
                next(stages, None)
    for stages, _ in riders:
        for _ in stages:
            pass


def _affine_stages(units, masks):
    lo = masks[0]["lane_lo"]
    bd = masks[0]["bd"]
    for u in units:
        u["aa"] = _dot_nt(_bf(jnp.concatenate([u["at"], u["rt"]], axis=0)),
                          jnp.concatenate([_block_diag(u["bt"], lo), _block_diag(u["kt"], lo)],
                                          axis=0))
    yield
    for u in units:
        m = masks[u["d"]]
        aa = u.pop("aa")
        u["n_ab"] = jnp.where(m["strict"], aa[:CHUNK, :PAIR], 0.0)
        u["a_ak"] = jnp.where(m["strict"], aa[:CHUNK, PAIR:], 0.0)
        u["a_rb"] = jnp.where(m["incl"], aa[CHUNK:, :PAIR], 0.0)
        u["a_rk"] = jnp.where(m["incl"], aa[CHUNK:, PAIR:], 0.0)
    for u in units:
        u["x"] = masks[u["d"]]["eye"] + u["n_ab"]
        u["pw"] = _dot(_bf(u["n_ab"]), _block_diag(u["n_ab"], lo))
    yield
    for u in units:
        av = _dot(_bf(jnp.concatenate([u["a_ak"], u["a_rk"]], axis=0)), _block_diag(u["v"], lo))
        u["akv"], u["rkv"] = av[:CHUNK], av[CHUNK:]
    yield
    width = 2
    while 2 * width < CHUNK:
        for u in units:
            zz = _dot(_bf(u["pw"]), jnp.concatenate([_block_diag(u["pw"], lo),
                                                     _block_diag(u["x"], lo)], axis=1))
            u["pw"] = zz[:, :PAIR]
            u["x"] = u["x"] + zz[:, PAIR:]
        yield
        width *= 2
    for u in units:
        u["x"] = u["x"] + _dot(_bf(u["pw"]), _block_diag(u["x"], lo))
    yield
    for u in units:
        wv = _dot(_bf(u["x"]), jnp.concatenate([_block_diag(u["at_true"], lo),
                                                _block_diag(u["akv"], lo)], axis=1))
        u["wt"], u["vt"] = wv[:, :PAIR], wv[:, PAIR:]
    yield
    for u in units:
        qy = _dot(_bf(u["a_rb"]), jnp.concatenate([_block_diag(u["wt"], lo),
                                                   _block_diag(u["vt"], lo)], axis=1))
        u["qhat"] = u["rt_true"] + qy[:, :PAIR]
        u["y0"] = qy[:, PAIR:] + u["rkv"]
    yield
    for u in units:
        u["mc"] = jnp.where(bd, _dot_tn(_bf(u["wt"]), _bf(u["bh"])), 0.0)
    yield
    for u in units:
        u["c0"] = jnp.where(bd, _dot_tn(_bf(jnp.concatenate([u["vt"], u["v"]], axis=0)),
                                        _bf(jnp.concatenate([u["bh"], u["kh"]], axis=0))), 0.0)
    yield


def _scan_prepare(direction, r, k, v, a, b, le, tmask):
    tt = r.shape[0]
    le_hi, le_lo = _split2(le)
    cum = _dot(tmask, le_hi) + _dot(tmask, le_lo)
    last = CHUNK - 1 if direction == 0 else 0
    tot = jnp.concatenate(
        [jnp.broadcast_to(cum[c * CHUNK + last:c * CHUNK + last + 1], (CHUNK, D_RWKV))
         for c in range(tt // CHUNK)], axis=0)
    off = 0.5 * tot
    ci = cum - off
    e_off = jnp.exp2(off)
    e_neg = jnp.exp2(-ci)
    tile = dict(at=a * jnp.exp2(ci - le), rt=r * jnp.exp2(ci), bt=b * e_neg, kt=k * e_neg, v=v)
    tile["bh"] = tile["bt"] * e_off
    tile["kh"] = tile["kt"] * e_off
    tile["at_true"] = tile["at"] * e_off
    tile["rt_true"] = tile["rt"] * e_off
    units = {}
    for c in range(tt // CHUNK):
        rs = slice(c * CHUNK, (c + 1) * CHUNK)
        for p in range(N_PAIR):
            sl = slice(p * PAIR, (p + 1) * PAIR)
            units[c, p] = dict({name: t[rs, sl] for name, t in tile.items()}, d=direction)
    return units, jnp.exp2(tot)


class _TileScan:
    def __init__(self, scans, states, post, masks):
        self.n_dir = len(scans)
        self.units, self.decay = [], []
        for d, scan in enumerate(scans):
            u, dec = _scan_prepare(d, *scan)
            self.units.append(u)
            self.decay.append(dec)
        self.n_chunk = self.decay[0].shape[0] // CHUNK
        self.states = states
        self.post = post
        self.masks = masks
        self.y_rows = [[None] * self.n_chunk for _ in scans]

    def chunk_of(self, d, j):
        return j if d == 0 else self.n_chunk - 1 - j

    def affine(self, steps):
        units = [self.units[d][self.chunk_of(d, j), p]
                 for j in steps for d in range(self.n_dir) for p in range(N_PAIR)]
        return _affine_stages(units, self.masks)

    def carry(self, j):
        for d in range(self.n_dir):
            c = self.chunk_of(d, j)
            ys = []
            for p in range(N_PAIR):
                u = self.units[d][c, p]
                s = self.states[d][p]
                s_bf = _bf(s)
                ys.append(_dot_nt(_bf(u["qhat"]), s_bf) + u["y0"])
                p_c = self.decay[d][c * CHUNK:c * CHUNK + 1, p * PAIR:(p + 1) * PAIR]
                self.states[d][p] = s * p_c + _dot(s_bf, _bf(u["mc"])) + u["c0"]
            self.y_rows[d][c] = jnp.concatenate(ys, axis=1)
        yield

    def finish(self, steps):
        stages = []
        for d in range(self.n_dir):
            cs = sorted(self.chunk_of(d, j) for j in steps)
            rows = slice(cs[0] * CHUNK, (cs[-1] + 1) * CHUNK)
            y = jnp.concatenate([self.y_rows[d][c] for c in cs], axis=0)
            stages.append(_post_stages(y, rows, *self.post[d]))
        for _ in range(3):
            for st in stages:
                next(st, None)
            yield


def _chain(*stage_iters):
    for it in stage_iters:
        yield from it


def _lockstep(stage_iters):
    live = list(stage_iters)
    while live:
        for it in list(live):
            try:
                next(it)
            except StopIteration:
                live.remove(it)
        if live:
            yield


def _pre_stages(direction, z, edge, mu, w0, wup, a0, aup, gup, k_k, k_a, ones_bd, out):
    tt = z.shape[0]
    row_id = lax.broadcasted_iota(jnp.int32, (HALO, 1), 0)
    if direction == 0:
        prev = pltpu.roll(z, 1, 0)
        prev = jnp.concatenate([jnp.where(row_id == 0, edge, prev[:HALO]), prev[HALO:]], axis=0)
    else:
        prev = pltpu.roll(z, tt - 1, 0)
        prev = jnp.concatenate([prev[:tt - HALO],
                                jnp.where(row_id == HALO - 1, edge, prev[tt - HALO:])], axis=0)
    zs = z + mu * (prev - z)
    r = zs[:, 0:D_RWKV]
    k = zs[:, D_RWKV:2 * D_RWKV]
    v = zs[:, 2 * D_RWKV:3 * D_RWKV]
    lora = zs[:, 3 * D_RWKV:3 * D_RWKV + LORA_WA]
    gl = zs[:, 3 * D_RWKV + LORA_WA:]

    w_lora = _dot(_bf(jnp.tanh(lora)), wup)
    a_lora = _dot(_bf(lora), aup)
    g = _dot(_bf(_sigmoid(gl)), gup)
    yield
    le = _sigmoid(w0 + w_lora) * LOG2_DECAY_SCALE
    a_sig = _sigmoid(a0 + a_lora)
    kk = k * k_k
    kk_sq = _seg_sum(kk * kk, ones_bd)
    yield
    kk = kk * lax.rsqrt(jnp.maximum(kk_sq, 1e-24))
    k = k * (1.0 + (a_sig - 1.0) * k_a)
    out.append((r, k, v, -kk, kk * a_sig, le, g))


def _post_stages(y, rows, r, k, v, g, r_k, ln_w, ln_b, ones_bd, o_ref, row0):
    mean = _seg_sum(y, ones_bd) * (1.0 / HEAD_DIM)
    rk_sum = _seg_sum(r[rows] * k[rows] * r_k, ones_bd)
    yield
    yc = y - mean
    var = _seg_sum(yc * yc, ones_bd) * (1.0 / HEAD_DIM)
    yield
    yn = yc * lax.rsqrt(var + GN_EPS) * ln_w + ln_b
    o_ref[0, row0 + rows.start:row0 + rows.stop, :] = (yn + rk_sum * v[rows]) * g[rows]


def _rwkv_kernel(zf_ref, hf_ref, zb_ref, hb_ref, mu_ref, w0_ref, wup_ref, a0_ref, aup_ref,
                 gup_ref, kk_ref, ka_ref, rk_ref, lnw_ref, lnb_ref, ones_ref, tmask_ref,
                 of_ref, ob_ref, s_ref):
    i = pl.program_id(1)

    @pl.when(i == 0)
    def _():
        s_ref[...] = jnp.zeros_like(s_ref)

    tt = zf_ref.shape[1] // 2
    ones_bd = ones_ref[...]
    masks = [_time_masks(0), _time_masks(1)]
    z_refs = (zf_ref, zb_ref)
    o_refs = (of_ref, ob_ref)
    states = [[s_ref[d, p] for p in range(N_PAIR)] for d in range(2)]
    row0 = ((0, tt), (tt, 0))

    def edge(sub, d):
        if sub == 0:
            halo = hf_ref[0, HALO - 1:HALO, :] if d == 0 else hb_ref[0, 0:1, :]
            return jnp.where(i > 0, halo, 0.0)
        return zf_ref[0, tt - 1:tt, :] if d == 0 else zb_ref[0, tt:tt + 1, :]

    def pre_stages(sub, out):
        return _lockstep([
            _pre_stages(d, z_refs[d][0, row0[sub][d]:row0[sub][d] + tt, :], edge(sub, d),
                        mu_ref[d], w0_ref[d], wup_ref[d], a0_ref[d], aup_ref[d], gup_ref[d],
                        kk_ref[d], ka_ref[d], ones_bd, out[d]) for d in range(2)])

    def tile_scan(sub, pre):
        pre = [p[0] for p in pre]
        post = [(pre[d][0], pre[d][1], pre[d][2], pre[d][6], rk_ref[d], lnw_ref[d], lnb_ref[d],
                 ones_bd, o_refs[d], row0[sub][d]) for d in range(2)]
        return _TileScan([pre[d][:6] + (tmask_ref[d],) for d in range(2)], states, post, masks)

    def run(stages):
        for _ in stages:
            pass

    def scan_first_half(t):
        half = t.n_chunk // 2
        run(t.affine(range(half)))
        _weave(t.affine(range(half, t.n_chunk)),
               *[(t.carry(j), (CARRY_AFTER[j],)) for j in range(half)])
        return range(half, t.n_chunk)

    pre0 = [[], []]
    run(pre_stages(0, pre0))
    t0 = tile_scan(0, pre0)
    rest = list(scan_first_half(t0))
    run(t0.carry(rest[0]))
    pre1 = [[], []]
    _weave(pre_stages(1, pre1),
           (_chain(*[t0.carry(j) for j in rest[1:]], t0.finish(range(t0.n_chunk))), (1, 2)))
    t1 = tile_scan(1, pre1)
    for j in scan_first_half(t1):
        run(t1.carry(j))
    run(t1.finish(range(t1.n_chunk)))
    for d in range(2):
        for p in range(N_PAIR):
            s_ref[d, p] = states[d][p]


def _chunk_sum_masks(tt):
    t = np.arange(tt)
    same = (t[:, None] // CHUNK) == (t[None, :] // CHUNK)
    fwd = same & (t[None, :] <= t[:, None])
    bwd = same & (t[None, :] >= t[:, None])
    return jnp.asarray(np.stack([fwd, bwd]), dtype=BF16)


def _rwkv(z_rwkv, mu, w0, wup, a0, aup, gup, k_k, k_a, r_k, ln_w, ln_b, ones_bd):
    bsz, seq, _ = z_rwkv.shape
    tt = 2 * TILE_SCAN
    n_tile = seq // tt
    blk = tt // HALO
    last_blk = n_tile * blk - 1

    def whole(arr):
        return pl.BlockSpec(arr.shape, lambda b, i: (0,) * arr.ndim)

    tmask = _chunk_sum_masks(TILE_SCAN)
    params = (mu, w0, wup, a0, aup, gup, k_k, k_a, r_k, ln_w, ln_b, ones_bd, tmask)
    tile = (1, tt, D_RWKV_IN)
    halo = (1, HALO, D_RWKV_IN)
    out_tile = (1, tt, D_RWKV)
    return pl.pallas_call(
        _rwkv_kernel,
        grid=(bsz, n_tile),
        in_specs=[pl.BlockSpec(tile, lambda b, i: (b, i, 0)),
                  pl.BlockSpec(halo, lambda b, i: (b, jnp.maximum(i * blk - 1, 0), 0)),
                  pl.BlockSpec(tile, lambda b, i: (b, n_tile - 1 - i, 0)),
                  pl.BlockSpec(halo, lambda b, i: (b, jnp.minimum((n_tile - i) * blk, last_blk), 0)),
                  ] + [whole(p) for p in params],
        out_specs=[pl.BlockSpec(out_tile, lambda b, i: (b, i, 0)),
                   pl.BlockSpec(out_tile, lambda b, i: (b, n_tile - 1 - i, 0))],
        out_shape=[jax.ShapeDtypeStruct((bsz, seq, D_RWKV), F32)] * 2,
        scratch_shapes=[pltpu.VMEM((2, N_PAIR, PAIR, PAIR), F32)],
        compiler_params=pltpu.CompilerParams(
            dimension_semantics=("arbitrary", "arbitrary"), vmem_limit_bytes=VMEM_LIMIT),
        name="rwkv7_scan",
    )(z_rwkv, z_rwkv, z_rwkv, z_rwkv, *params)


def _out_kernel(x_ref, zc_ref, zp_ref, zn_ref, of_ref, ob_ref, mod_ref, cw_ref, g_post_mix_ref,
                g_pre_ffn_ref, g_post_ffn_ref, wout_ref, w1_ref, w2_ref, y_ref):
    i = pl.program_id(1)
    n = pl.num_programs(1)
    tm = x_ref.shape[1]
    m = mod_ref[0]
    cw = cw_ref[...]
    blocks = _row_blocks(tm)
    row_id = lax.broadcasted_iota(jnp.int32, (HALO, 1), 0)

    def gated(zrows):
        return zrows[:, D_CONV:2 * D_CONV] * zrows[:, 2 * D_CONV:]

    def mixer_inputs(rows):
        zc = zc_ref[0, rows, :]
        cu = gated(zc)
        if rows.start == 0:
            cu_prev = jnp.where(i > 0, gated(zp_ref[0, HALO - 1:HALO, :]), 0.0)
        else:
            cu_prev = gated(zc_ref[0, rows.start - 1:rows.start, :])
        if rows.stop == tm:
            cu_next = jnp.where(i < n - 1, gated(zn_ref[0, 0:1, :]), 0.0)
        else:
            cu_next = gated(zc_ref[0, rows.stop:rows.stop + 1, :])
        before = pltpu.roll(cu, 1, 0)
        before = jnp.concatenate([jnp.where(row_id == 0, cu_prev, before[:HALO]), before[HALO:]],
                                 axis=0)
        after = pltpu.roll(cu, ROW_BLOCK - 1, 0)
        after = jnp.concatenate([after[:ROW_BLOCK - HALO],
                                 jnp.where(row_id == HALO - 1, cu_next, after[ROW_BLOCK - HALO:])],
                                axis=0)
        y_conv = zc[:, 0:D_CONV] * (cw[0:1] * before + cw[1:2] * cu + cw[2:3] * after)
        y_rwkv = of_ref[0, rows, :] + ob_ref[0, rows, :]
        return y_conv.astype(BF16), y_rwkv.astype(BF16)

    def out_proj(inputs):
        y_conv, y_rwkv = inputs
        return _dot(y_conv, wout_ref[:D_CONV, :]) + _dot(y_rwkv, wout_ref[D_CONV:, :])

    def mlp_input(rows, mix):
        x1 = x_ref[0, rows, :] + m[2:3] * _rms(mix, g_post_mix_ref[...])
        h = _rms(x1, g_pre_ffn_ref[...]) * (1.0 + m[4:5]) + m[3:4]
        return x1, h.astype(BF16)

    def mlp(hb):
        f = None
        for j in range(0, D_FF, FF_CHUNK):
            hid = jnp.square(jnp.maximum(_dot(hb, w1_ref[:, j:j + FF_CHUNK]), 0.0))
            part = _dot(hid.astype(BF16), w2_ref[j:j + FF_CHUNK, :])
            f = part if f is None else f + part
        return f

    mixes = []
    for rows in blocks:
        mixes.append(out_proj(mixer_inputs(rows)))
    mid = [mlp_input(rows, mix) for rows, mix in zip(blocks, mixes)]
    for rows, (x1, hb) in zip(blocks, mid):
        y_ref[0, rows, :] = x1 + m[5:6] * _rms(mlp(hb), g_post_ffn_ref[...])


def _outproj_ffn(x, z_conv, o_fwd, o_bwd, mod, conv_w_pad, g_post_mix, g_pre_ffn, g_post_ffn,
                 w_out_bf, w1_bf, w2_bf):
    bsz, seq, _ = x.shape
    tm = TILE_OUT
    n_tile = seq // tm
    blk = tm // HALO

    def const(shape):
        return pl.BlockSpec(shape, lambda b, i: (0,) * len(shape),
                            pipeline_mode=pl.Buffered(1))

    return pl.pallas_call(
        _out_kernel,
        grid=(bsz, n_tile),
        in_specs=[pl.BlockSpec((1, tm, D_MODEL), lambda b, i: (b, i, 0)),
                  pl.BlockSpec((1, tm, D_CONV_IN), lambda b, i: (b, i, 0)),
                  pl.BlockSpec((1, HALO, D_CONV_IN),
                               lambda b, i: (b, jnp.maximum(i * blk - 1, 0), 0)),
                  pl.BlockSpec((1, HALO, D_CONV_IN),
                               lambda b, i: (b, jnp.minimum((i + 1) * blk, n_tile * blk - 1), 0)),
                  pl.BlockSpec((1, tm, D_RWKV), lambda b, i: (b, i, 0)),
                  pl.BlockSpec((1, tm, D_RWKV), lambda b, i: (b, i, 0)),
                  pl.BlockSpec((1, 6, D_MODEL), lambda b, i: (b, 0, 0)),
                  const((HALO, D_CONV)), const((1, D_MODEL)), const((1, D_MODEL)),
                  const((1, D_MODEL)), const((D_MODEL, D_MODEL)), const((D_MODEL, D_FF)),
                  const((D_FF, D_MODEL))],
        out_specs=pl.BlockSpec((1, tm, D_MODEL), lambda b, i: (b, i, 0)),
        out_shape=jax.ShapeDtypeStruct((bsz, seq, D_MODEL), F32),
        compiler_params=pltpu.CompilerParams(
            dimension_semantics=("arbitrary", "arbitrary"), vmem_limit_bytes=VMEM_LIMIT),
        name="out_proj_ffn",
    )(x, z_conv, z_conv, z_conv, o_fwd, o_bwd, mod, conv_w_pad, g_post_mix, g_pre_ffn,
      g_post_ffn, w_out_bf, w1_bf, w2_bf)


def _lora_weight(w, top):
    return _bf(jnp.pad(w, ((0, 0), (top, LORA_WA - top - w.shape[1]), (0, 0))))


def _layer(xs, cs, w_ada, b_ada, g_pre_mix, g_post_mix, w_in, conv_w, mu_shift, w0, w_up, a0,
           a_up, g_up, k_k, k_a, r_k, ln_x_w, ln_x_b, w_out, g_pre_ffn, g_post_ffn, w_ffn1,
           w_ffn2):
    n_seq = sum(c.shape[0] for c in cs)
    c_all = jnp.pad(jnp.concatenate(cs, axis=0), ((0, HALO - n_seq), (0, 0)))
    mod = _modulation(c_all, w_ada, b_ada[None, :]).reshape(HALO, 6, D_MODEL)

    w_in_bf = w_in.astype(BF16)
    w_out_bf = w_out.astype(BF16)
    w1_bf = w_ffn1.astype(BF16)
    w2_bf = w_ffn2.astype(BF16)
    row = lambda p: p[None, :]
    dir_row = lambda p: p.reshape(2, 1, -1)
    wup = _lora_weight(w_up, 0)
    aup = _lora_weight(a_up, LORA_WA // 2)
    gup = _lora_weight(g_up, 0)
    conv_w_pad = jnp.pad(conv_w, ((0, HALO - conv_w.shape[0]), (0, 0)))
    head = jnp.arange(SEG_LANES) // HEAD_DIM
    ones_bd = (head[:, None] == head[None, :]).astype(BF16)

    outs = []
    start = 0
    for x in xs:
        bsz = x.shape[0]
        m = mod[start:start + bsz]
        start += bsz
        z_conv, z_rwkv = _inproj(x, m, row(g_pre_mix), w_in_bf)
        o_fwd, o_bwd = _rwkv(z_rwkv, dir_row(mu_shift), dir_row(w0), wup, dir_row(a0), aup, gup,
                             dir_row(k_k), dir_row(k_a), dir_row(r_k), dir_row(ln_x_w),
                             dir_row(ln_x_b), ones_bd)
        outs.append(_outproj_ffn(x, z_conv, o_fwd, o_bwd, m, conv_w_pad, row(g_post_mix),
                                 row(g_pre_ffn), row(g_post_ffn), w_out_bf, w1_bf, w2_bf))
    return outs


def kernel(x_prompt, x_sample, c_prompt, c_sample, w_ada, b_ada, g_pre_mix, g_post_mix, w_in,
           conv_w, mu_shift, w0, w_up, a0, a_up, g_up, k_k, k_a, r_k, ln_x_w, ln_x_b, w_out,
           g_pre_ffn, g_post_ffn, w_ffn1, w_ffn2):
    xs = [x_prompt, x_sample]
    for l in range(w_ada.shape[0]):
        xs = _layer(xs, [c_prompt, c_sample], w_ada[l], b_ada[l], g_pre_mix[l], g_post_mix[l],
                    w_in[l], conv_w[l], mu_shift[l], w0[l], w_up[l], a0[l], a_up[l], g_up[l],
                    k_k[l], k_a[l], r_k[l], ln_x_w[l], ln_x_b[l], w_out[l], g_pre_ffn[l],
                    g_post_ffn[l], w_ffn1[l], w_ffn2[l])
    return (xs[0], xs[1])
```

```python
import math

import numpy as np

import jax
import jax.numpy as jnp
from jax import lax
from jax.experimental import pallas as pl
from jax.experimental.pallas import tpu as pltpu

D_MODEL = 1024
D_CONV = 512
D_RWKV = 512
HEAD_DIM = 64
LORA_WA = 128
GATE_RANK = 128
D_RWKV_IN = 3 * D_RWKV + LORA_WA + GATE_RANK
D_CONV_IN = 3 * D_CONV
D_CONV_MIX = 2 * D_CONV
D_FF = 4096
NORM_EPS = 1e-6
GN_EPS = 64e-5
LOG2_DECAY_SCALE = -math.exp(-0.5) * math.log2(math.e)

CHUNK = 64
PAIR = 2 * HEAD_DIM
N_PAIR = D_RWKV // PAIR
HALO = 8
SEG_LANES = 256

TILE_IN = 1024
TILE_SCAN = 256
TILE_OUT = 1024
ROW_BLOCK = 256
FF_CHUNK = 1024
CARRY_AFTER = (3, 8)
VMEM_LIMIT = 56 * 1024 * 1024
VMEM_LIMIT_OUT = 60 * 1024 * 1024

F32 = jnp.float32
BF16 = jnp.bfloat16


def _dot(a, b):
    return jnp.dot(a, b, preferred_element_type=F32)


def _dot_nt(a, b):
    return lax.dot_general(a, b, (((1,), (1,)), ((), ())), preferred_element_type=F32)


def _dot_tn(a, b):
    return lax.dot_general(a, b, (((0,), (0,)), ((), ())), preferred_element_type=F32)


def _bf(x):
    return x.astype(BF16)


def _split2(x):
    hi = x.astype(BF16)
    return hi, (x - hi.astype(F32)).astype(BF16)


def _sigmoid(x):
    return 1.0 / (1.0 + jnp.exp(-x))


def _rms(x, g):
    return x * lax.rsqrt(jnp.mean(x * x, axis=-1, keepdims=True) + NORM_EPS) * g


def _seg_sum(x, ones_bd):
    xb = _bf(x)
    return jnp.concatenate([_dot(xb[:, j:j + SEG_LANES], ones_bd)
                            for j in range(0, x.shape[1], SEG_LANES)], axis=1)


def _row_blocks(rows):
    return [slice(j, j + ROW_BLOCK) for j in range(0, rows, ROW_BLOCK)]


def _mod_kernel(c_ref, w_ref, b_ref, o_ref):
    c = c_ref[...]
    s = c * _sigmoid(c)
    rows = s.shape[0]
    s_hi, s_lo = _split2(s)
    w_hi, w_lo = _split2(w_ref[...])
    hi = _dot(jnp.concatenate([s_hi, s_lo], axis=0), w_hi)
    o_ref[...] = hi[:rows] + hi[rows:] + _dot(s_hi, w_lo) + b_ref[...]


def _modulation(c_all, w_ada, b_ada):
    rows = c_all.shape[0]
    n_blk = w_ada.shape[1] // D_MODEL
    return pl.pallas_call(
        _mod_kernel,
        grid=(n_blk,),
        in_specs=[pl.BlockSpec((rows, D_MODEL), lambda j: (0, 0)),
                  pl.BlockSpec((D_MODEL, D_MODEL), lambda j: (0, j)),
                  pl.BlockSpec((1, D_MODEL), lambda j: (0, j))],
        out_specs=pl.BlockSpec((rows, D_MODEL), lambda j: (0, j)),
        out_shape=jax.ShapeDtypeStruct((rows, w_ada.shape[1]), F32),
        name="adaln_mod",
    )(c_all, w_ada, b_ada)


def _inproj_kernel(x_ref, mod_ref, g_ref, w_ref, zc_ref, zr_ref):
    m = mod_ref[0]
    for rows in _row_blocks(x_ref.shape[1]):
        h = _rms(x_ref[0, rows, :], g_ref[...]) * (1.0 + m[1:2]) + m[0:1]
        hb = h.astype(BF16)
        zc = _dot(hb, w_ref[:, :D_CONV_IN])
        zc_ref[0, rows, :D_CONV] = zc[:, :D_CONV]
        zc_ref[0, rows, D_CONV:] = zc[:, D_CONV:2 * D_CONV] * zc[:, 2 * D_CONV:]
        zr_ref[0, rows, :] = _dot(hb, w_ref[:, D_CONV_IN:])


def _inproj(x, mod, g_pre, w_in_bf):
    bsz, seq, _ = x.shape
    tm = TILE_IN
    d_in = w_in_bf.shape[1]
    return pl.pallas_call(
        _inproj_kernel,
        grid=(bsz, seq // tm),
        in_specs=[pl.BlockSpec((1, tm, D_MODEL), lambda b, i: (b, i, 0)),
                  pl.BlockSpec((1, 6, D_MODEL), lambda b, i: (b, 0, 0)),
                  pl.BlockSpec((1, D_MODEL), lambda b, i: (0, 0)),
                  pl.BlockSpec((D_MODEL, d_in), lambda b, i: (0, 0),
                               pipeline_mode=pl.Buffered(1))],
        out_specs=[pl.BlockSpec((1, tm, D_CONV_MIX), lambda b, i: (b, i, 0)),
                   pl.BlockSpec((1, tm, D_RWKV_IN), lambda b, i: (b, i, 0))],
        out_shape=[jax.ShapeDtypeStruct((bsz, seq, D_CONV_MIX), F32),
                   jax.ShapeDtypeStruct((bsz, seq, D_RWKV_IN), F32)],
        compiler_params=pltpu.CompilerParams(
            dimension_semantics=("arbitrary", "arbitrary"), vmem_limit_bytes=VMEM_LIMIT),
        name="in_proj",
    )(x, mod, g_pre, w_in_bf)


def _time_masks(direction):
    sign = 1 if direction == 0 else -1
    row = lax.broadcasted_iota(jnp.int32, (CHUNK, PAIR), 0)
    lane = lax.broadcasted_iota(jnp.int32, (CHUNK, PAIR), 1)
    delta = (row - jnp.bitwise_and(lane, HEAD_DIM - 1)) * sign
    q_row = lax.broadcasted_iota(jnp.int32, (PAIR, PAIR), 0)
    q_lane = lax.broadcasted_iota(jnp.int32, (PAIR, PAIR), 1)
    return dict(strict=delta > 0, incl=delta >= 0,
                eye=jnp.where(delta == 0, 1.0, 0.0).astype(F32),
                lane_lo=lane < HEAD_DIM,
                bd=(q_row < HEAD_DIM) == (q_lane < HEAD_DIM))


def _block_diag(y, lane_lo):
    return _bf(jnp.concatenate([jnp.where(lane_lo, y, 0.0), jnp.where(lane_lo, 0.0, y)], axis=0))


def _weave(main, *riders):
    n = 0
    for _ in main:
        n += 1
        for stages, positions in riders:
            for _ in range(positions.count(n)):
                next(stages, None)
    for stages, _ in riders:
        for _ in stages:
            pass


def _affine_stages(units, masks):
    lo = masks[0]["lane_lo"]
    bd = masks[0]["bd"]
    for u in units:
        u["aa"] = _dot_nt(_bf(jnp.concatenate([u["at"], u["rt"]], axis=0)),
                          jnp.concatenate([_block_diag(u["bt"], lo), _block_diag(u["kt"], lo)],
                                          axis=0))
    yield
    for u in units:
        m = masks[u["d"]]
        aa = u.pop("aa")
        u["n_ab"] = jnp.where(m["strict"], aa[:CHUNK, :PAIR], 0.0)
        u["a_ak"] = jnp.where(m["strict"], aa[:CHUNK, PAIR:], 0.0)
        u["a_rb"] = jnp.where(m["incl"], aa[CHUNK:, :PAIR], 0.0)
        u["a_rk"] = jnp.where(m["incl"], aa[CHUNK:, PAIR:], 0.0)
    for u in units:
        u["x"] = masks[u["d"]]["eye"] + u["n_ab"]
        u["pw"] = _dot(_bf(u["n_ab"]), _block_diag(u["n_ab"], lo))
    yield
    for u in units:
        av = _dot(_bf(jnp.concatenate([u["a_ak"], u["a_rk"]], axis=0)), _block_diag(u["v"], lo))
        u["akv"], u["rkv"] = av[:CHUNK], av[CHUNK:]
    yield
    width = 2
    while 2 * width < CHUNK:
        for u in units:
            zz = _dot(_bf(u["pw"]), jnp.concatenate([_block_diag(u["pw"], lo),
                                                     _block_diag(u["x"], lo)], axis=1))
            u["pw"] = zz[:, :PAIR]
            u["x"] = u["x"] + zz[:, PAIR:]
        yield
        width *= 2
    for u in units:
        u["x"] = u["x"] + _dot(_bf(u["pw"]), _block_diag(u["x"], lo))
    yield
    for u in units:
        wv = _dot(_bf(u["x"]), jnp.concatenate([_block_diag(u["at_true"], lo),
                                                _block_diag(u["akv"], lo)], axis=1))
        u["wt"], u["vt"] = wv[:, :PAIR], wv[:, PAIR:]
    yield
    for u in units:
        qy = _dot(_bf(u["a_rb"]), jnp.concatenate([_block_diag(u["wt"], lo),
                                                   _block_diag(u["vt"], lo)], axis=1))
        u["qhat"] = u["rt_true"] + qy[:, :PAIR]
        u["y0"] = qy[:, PAIR:] + u["rkv"]
    yield
    for u in units:
        u["mc"] = jnp.where(bd, _dot_tn(_bf(u["wt"]), _bf(u["bh"])), 0.0)
    yield
    for u in units:
        u["c0"] = jnp.where(bd, _dot_tn(_bf(jnp.concatenate([u["vt"], u["v"]], axis=0)),
                                        _bf(jnp.concatenate([u["bh"], u["kh"]], axis=0))), 0.0)
    yield


def _scan_prepare(direction, r, k, v, a, b, le, tmask):
    tt = r.shape[0]
    le_hi, le_lo = _split2(le)
    cum = _dot(tmask, le_hi) + _dot(tmask, le_lo)
    last = CHUNK - 1 if direction == 0 else 0
    tot = jnp.concatenate(
        [jnp.broadcast_to(cum[c * CHUNK + last:c * CHUNK + last + 1], (CHUNK, D_RWKV))
         for c in range(tt // CHUNK)], axis=0)
    off = 0.5 * tot
    ci = cum - off
    e_off = jnp.exp2(off)
    e_neg = jnp.exp2(-ci)
    tile = dict(at=a * jnp.exp2(ci - le), rt=r * jnp.exp2(ci), bt=b * e_neg, kt=k * e_neg, v=v)
    tile["bh"] = tile["bt"] * e_off
    tile["kh"] = tile["kt"] * e_off
    tile["at_true"] = tile["at"] * e_off
    tile["rt_true"] = tile["rt"] * e_off
    units = {}
    for c in range(tt // CHUNK):
        rs = slice(c * CHUNK, (c + 1) * CHUNK)
        for p in range(N_PAIR):
            sl = slice(p * PAIR, (p + 1) * PAIR)
            units[c, p] = dict({name: t[rs, sl] for name, t in tile.items()}, d=direction)
    return units, jnp.exp2(tot)


class _TileScan:
    def __init__(self, scans, states, post, masks):
        self.n_dir = len(scans)
        self.units, self.decay = [], []
        for d, scan in enumerate(scans):
            u, dec = _scan_prepare(d, *scan)
            self.units.append(u)
            self.decay.append(dec)
        self.n_chunk = self.decay[0].shape[0] // CHUNK
        self.states = states
        self.post = post
        self.masks = masks
        self.y_rows = [[None] * self.n_chunk for _ in scans]

    def chunk_of(self, d, j):
        return j if d == 0 else self.n_chunk - 1 - j

    def affine(self, steps):
        units = [self.units[d][self.chunk_of(d, j), p]
                 for j in steps for d in range(self.n_dir) for p in range(N_PAIR)]
        return _affine_stages(units, self.masks)

    def carry(self, j):
        for d in range(self.n_dir):
            c = self.chunk_of(d, j)
            ys = []
            for p in range(N_PAIR):
                u = self.units[d][c, p]
                s = self.states[d][p]
                s_bf = _bf(s)
                ys.append(_dot_nt(_bf(u["qhat"]), s_bf) + u["y0"])
                p_c = self.decay[d][c * CHUNK:c * CHUNK + 1, p * PAIR:(p + 1) * PAIR]
                self.states[d][p] = s * p_c + _dot(s_bf, _bf(u["mc"])) + u["c0"]
            self.y_rows[d][c] = jnp.concatenate(ys, axis=1)
        yield

    def finish(self, steps):
        stages = []
        for d in range(self.n_dir):
            cs = sorted(self.chunk_of(d, j) for j in steps)
            rows = slice(cs[0] * CHUNK, (cs[-1] + 1) * CHUNK)
            y = jnp.concatenate([self.y_rows[d][c] for c in cs], axis=0)
            stages.append(_post_stages(y, rows, *self.post[d]))
        for _ in range(3):
            for st in stages:
                next(st, None)
            yield


def _chain(*stage_iters):
    for it in stage_iters:
        yield from it


def _lockstep(stage_iters):
    live = list(stage_iters)
    while live:
        for it in list(live):
            try:
                next(it)
            except StopIteration:
                live.remove(it)
        if live:
            yield


def _pre_stages(direction, z, edge, mu, w0, wup, a0, aup, gup, k_k, k_a, ones_bd, out):
    tt = z.shape[0]
    row_id = lax.broadcasted_iota(jnp.int32, (HALO, 1), 0)
    if direction == 0:
        prev = pltpu.roll(z, 1, 0)
        prev = jnp.concatenate([jnp.where(row_id == 0, edge, prev[:HALO]), prev[HALO:]], axis=0)
    else:
        prev = pltpu.roll(z, tt - 1, 0)
        prev = jnp.concatenate([prev[:tt - HALO],
                                jnp.where(row_id == HALO - 1, edge, prev[tt - HALO:])], axis=0)
    zs = z + mu * (prev - z)
    r = zs[:, 0:D_RWKV]
    k = zs[:, D_RWKV:2 * D_RWKV]
    v = zs[:, 2 * D_RWKV:3 * D_RWKV]
    lora = zs[:, 3 * D_RWKV:3 * D_RWKV + LORA_WA]
    gl = zs[:, 3 * D_RWKV + LORA_WA:]

    w_lora = _dot(_bf(jnp.tanh(lora)), wup)
    a_lora = _dot(_bf(lora), aup)
    g = _dot(_bf(_sigmoid(gl)), gup)
    yield
    le = _sigmoid(w0 + w_lora) * LOG2_DECAY_SCALE
    a_sig = _sigmoid(a0 + a_lora)
    kk = k * k_k
    kk_sq = _seg_sum(kk * kk, ones_bd)
    yield
    kk = kk * lax.rsqrt(jnp.maximum(kk_sq, 1e-24))
    k = k * (1.0 + (a_sig - 1.0) * k_a)
    out.append((r, k, v, -kk, kk * a_sig, le, g))


def _post_stages(y, rows, r, k, v, g, r_k, ln_w, ln_b, ones_bd, o_ref, row0):
    mean = _seg_sum(y, ones_bd) * (1.0 / HEAD_DIM)
    rk_sum = _seg_sum(r[rows] * k[rows] * r_k, ones_bd)
    yield
    yc = y - mean
    var = _seg_sum(yc * yc, ones_bd) * (1.0 / HEAD_DIM)
    yield
    yn = yc * lax.rsqrt(var + GN_EPS) * ln_w + ln_b
    o_ref[0, row0 + rows.start:row0 + rows.stop, :] = (yn + rk_sum * v[rows]) * g[rows]


def _rwkv_kernel(zf_ref, hf_ref, zb_ref, hb_ref, mu_ref, w0_ref, wup_ref, a0_ref, aup_ref,
                 gup_ref, kk_ref, ka_ref, rk_ref, lnw_ref, lnb_ref, ones_ref, tmask_ref,
                 of_ref, ob_ref, s_ref):
    i = pl.program_id(1)

    @pl.when(i == 0)
    def _():
        s_ref[...] = jnp.zeros_like(s_ref)

    tt = zf_ref.shape[1] // 2
    ones_bd = ones_ref[...]
    masks = [_time_masks(0), _time_masks(1)]
    z_refs = (zf_ref, zb_ref)
    o_refs = (of_ref, ob_ref)
    states = [[s_ref[d, p] for p in range(N_PAIR)] for d in range(2)]
    row0 = ((0, tt), (tt, 0))

    def edge(sub, d):
        if sub == 0:
            halo = hf_ref[0, HALO - 1:HALO, :] if d == 0 else hb_ref[0, 0:1, :]
            return jnp.where(i > 0, halo, 0.0)
        return zf_ref[0, tt - 1:tt, :] if d == 0 else zb_ref[0, tt:tt + 1, :]

    def pre_stages(sub, out):
        return _lockstep([
            _pre_stages(d, z_refs[d][0, row0[sub][d]:row0[sub][d] + tt, :], edge(sub, d),
                        mu_ref[d], w0_ref[d], wup_ref[d], a0_ref[d], aup_ref[d], gup_ref[d],
                        kk_ref[d], ka_ref[d], ones_bd, out[d]) for d in range(2)])

    def tile_scan(sub, pre):
        pre = [p[0] for p in pre]
        post = [(pre[d][0], pre[d][1], pre[d][2], pre[d][6], rk_ref[d], lnw_ref[d], lnb_ref[d],
                 ones_bd, o_refs[d], row0[sub][d]) for d in range(2)]
        return _TileScan([pre[d][:6] + (tmask_ref[d],) for d in range(2)], states, post, masks)

    def run(stages):
        for _ in stages:
            pass

    def scan_first_half(t):
        half = t.n_chunk // 2
        run(t.affine(range(half)))
        _weave(t.affine(range(half, t.n_chunk)),
               *[(t.carry(j), (CARRY_AFTER[j],)) for j in range(half)])
        return range(half, t.n_chunk)

    pre0 = [[], []]
    run(pre_stages(0, pre0))
    t0 = tile_scan(0, pre0)
    rest = list(scan_first_half(t0))
    run(t0.carry(rest[0]))
    pre1 = [[], []]
    _weave(pre_stages(1, pre1),
           (_chain(*[t0.carry(j) for j in rest[1:]], t0.finish(range(t0.n_chunk))), (1, 2)))
    t1 = tile_scan(1, pre1)
    for j in scan_first_half(t1):
        run(t1.carry(j))
    run(t1.finish(range(t1.n_chunk)))
    for d in range(2):
        for p in range(N_PAIR):
            s_ref[d, p] = states[d][p]


def _chunk_sum_masks(tt):
    t = np.arange(tt)
    same = (t[:, None] // CHUNK) == (t[None, :] // CHUNK)
    fwd = same & (t[None, :] <= t[:, None])
    bwd = same & (t[None, :] >= t[:, None])
    return jnp.asarray(np.stack([fwd, bwd]), dtype=BF16)


def _rwkv(z_rwkv, mu, w0, wup, a0, aup, gup, k_k, k_a, r_k, ln_w, ln_b, ones_bd):
    bsz, seq, _ = z_rwkv.shape
    tt = 2 * TILE_SCAN
    n_tile = seq // tt
    blk = tt // HALO
    last_blk = n_tile * blk - 1

    def whole(arr):
        return pl.BlockSpec(arr.shape, lambda b, i: (0,) * arr.ndim)

    tmask = _chunk_sum_masks(TILE_SCAN)
    params = (mu, w0, wup, a0, aup, gup, k_k, k_a, r_k, ln_w, ln_b, ones_bd, tmask)
    tile = (1, tt, D_RWKV_IN)
    halo = (1, HALO, D_RWKV_IN)
    out_tile = (1, tt, D_RWKV)
    return pl.pallas_call(
        _rwkv_kernel,
        grid=(bsz, n_tile),
        in_specs=[pl.BlockSpec(tile, lambda b, i: (b, i, 0)),
                  pl.BlockSpec(halo, lambda b, i: (b, jnp.maximum(i * blk - 1, 0), 0)),
                  pl.BlockSpec(tile, lambda b, i: (b, n_tile - 1 - i, 0)),
                  pl.BlockSpec(halo, lambda b, i: (b, jnp.minimum((n_tile - i) * blk, last_blk), 0)),
                  ] + [whole(p) for p in params],
        out_specs=[pl.BlockSpec(out_tile, lambda b, i: (b, i, 0)),
                   pl.BlockSpec(out_tile, lambda b, i: (b, n_tile - 1 - i, 0))],
        out_shape=[jax.ShapeDtypeStruct((bsz, seq, D_RWKV), F32)] * 2,
        scratch_shapes=[pltpu.VMEM((2, N_PAIR, PAIR, PAIR), F32)],
        compiler_params=pltpu.CompilerParams(
            dimension_semantics=("arbitrary", "arbitrary"), vmem_limit_bytes=VMEM_LIMIT),
        name="rwkv7_scan",
    )(z_rwkv, z_rwkv, z_rwkv, z_rwkv, *params)


def _out_kernel(x_ref, zc_ref, zp_ref, zn_ref, of_ref, ob_ref, mod_ref, cw_ref, g_post_mix_ref,
                g_pre_ffn_ref, g_post_ffn_ref, wout_ref, w1_ref, w2_ref, y_ref):
    i = pl.program_id(1)
    n = pl.num_programs(1)
    tm = x_ref.shape[1]
    m = mod_ref[0]
    cw = cw_ref[...]
    blocks = _row_blocks(tm)
    row_id = lax.broadcasted_iota(jnp.int32, (HALO, 1), 0)

    def gated(zrows):
        return zrows[:, D_CONV:]

    def mixer_inputs(rows):
        zc = zc_ref[0, rows, :]
        cu = gated(zc)
        if rows.start == 0:
            cu_prev = jnp.where(i > 0, gated(zp_ref[0, HALO - 1:HALO, :]), 0.0)
        else:
            cu_prev = gated(zc_ref[0, rows.start - 1:rows.start, :])
        if rows.stop == tm:
            cu_next = jnp.where(i < n - 1, gated(zn_ref[0, 0:1, :]), 0.0)
        else:
            cu_next = gated(zc_ref[0, rows.stop:rows.stop + 1, :])
        before = pltpu.roll(cu, 1, 0)
        before = jnp.concatenate([jnp.where(row_id == 0, cu_prev, before[:HALO]), before[HALO:]],
                                 axis=0)
        after = pltpu.roll(cu, ROW_BLOCK - 1, 0)
        after = jnp.concatenate([after[:ROW_BLOCK - HALO],
                                 jnp.where(row_id == HALO - 1, cu_next, after[ROW_BLOCK - HALO:])],
                                axis=0)
        y_conv = zc[:, 0:D_CONV] * (cw[0:1] * before + cw[1:2] * cu + cw[2:3] * after)
        y_rwkv = of_ref[0, rows, :] + ob_ref[0, rows, :]
        return y_conv.astype(BF16), y_rwkv.astype(BF16)

    def out_proj(inputs):
        y_conv, y_rwkv = inputs
        return _dot(y_conv, wout_ref[:D_CONV, :]) + _dot(y_rwkv, wout_ref[D_CONV:, :])

    def mlp_input(rows, mix):
        x1 = x_ref[0, rows, :] + m[2:3] * _rms(mix, g_post_mix_ref[...])
        h = _rms(x1, g_pre_ffn_ref[...]) * (1.0 + m[4:5]) + m[3:4]
        return x1, h.astype(BF16)

    def mlp(hb):
        f = None
        for j in range(0, D_FF, FF_CHUNK):
            hid = jnp.square(jnp.maximum(_dot(hb, w1_ref[:, j:j + FF_CHUNK]), 0.0))
            part = _dot(hid.astype(BF16), w2_ref[j:j + FF_CHUNK, :])
            f = part if f is None else f + part
        return f

    mixes = []
    for rows in blocks:
        mixes.append(out_proj(mixer_inputs(rows)))
    mid = [mlp_input(rows, mix) for rows, mix in zip(blocks, mixes)]
    for rows, (x1, hb) in zip(blocks, mid):
        y_ref[0, rows, :] = x1 + m[5:6] * _rms(mlp(hb), g_post_ffn_ref[...])


def _outproj_ffn(x, z_conv, o_fwd, o_bwd, mod, conv_w_pad, g_post_mix, g_pre_ffn, g_post_ffn,
                 w_out_bf, w1_bf, w2_bf):
    bsz, seq, _ = x.shape
    tm = TILE_OUT
    n_tile = seq // tm
    blk = tm // HALO

    def const(shape):
        return pl.BlockSpec(shape, lambda b, i: (0,) * len(shape),
                            pipeline_mode=pl.Buffered(1))

    return pl.pallas_call(
        _out_kernel,
        grid=(bsz, n_tile),
        in_specs=[pl.BlockSpec((1, tm, D_MODEL), lambda b, i: (b, i, 0)),
                  pl.BlockSpec((1, tm, D_CONV_MIX), lambda b, i: (b, i, 0)),
                  pl.BlockSpec((1, HALO, D_CONV_MIX),
                               lambda b, i: (b, jnp.maximum(i * blk - 1, 0), 0)),
                  pl.BlockSpec((1, HALO, D_CONV_MIX),
                               lambda b, i: (b, jnp.minimum((i + 1) * blk, n_tile * blk - 1), 0)),
                  pl.BlockSpec((1, tm, D_RWKV), lambda b, i: (b, i, 0)),
                  pl.BlockSpec((1, tm, D_RWKV), lambda b, i: (b, i, 0)),
                  pl.BlockSpec((1, 6, D_MODEL), lambda b, i: (b, 0, 0)),
                  const((HALO, D_CONV)), const((1, D_MODEL)), const((1, D_MODEL)),
                  const((1, D_MODEL)), const((D_MODEL, D_MODEL)), const((D_MODEL, D_FF)),
                  const((D_FF, D_MODEL))],
        out_specs=pl.BlockSpec((1, tm, D_MODEL), lambda b, i: (b, i, 0)),
        out_shape=jax.ShapeDtypeStruct((bsz, seq, D_MODEL), F32),
        compiler_params=pltpu.CompilerParams(
            dimension_semantics=("arbitrary", "arbitrary"), vmem_limit_bytes=VMEM_LIMIT_OUT),
        name="out_proj_ffn",
    )(x, z_conv, z_conv, z_conv, o_fwd, o_bwd, mod, conv_w_pad, g_post_mix, g_pre_ffn,
      g_post_ffn, w_out_bf, w1_bf, w2_bf)


def _lora_weight(w, top):
    return _bf(jnp.pad(w, ((0, 0), (top, LORA_WA - top - w.shape[1]), (0, 0))))


def _layer(xs, cs, w_ada, b_ada, g_pre_mix, g_post_mix, w_in, conv_w, mu_shift, w0, w_up, a0,
           a_up, g_up, k_k, k_a, r_k, ln_x_w, ln_x_b, w_out, g_pre_ffn, g_post_ffn, w_ffn1,
           w_ffn2):
    n_seq = sum(c.shape[0] for c in cs)
    c_all = jnp.pad(jnp.concatenate(cs, axis=0), ((0, HALO - n_seq), (0, 0)))
    mod = _modulation(c_all, w_ada, b_ada[None, :]).reshape(HALO, 6, D_MODEL)

    w_in_bf = w_in.astype(BF16)
    w_out_bf = w_out.astype(BF16)
    w1_bf = w_ffn1.astype(BF16)
    w2_bf = w_ffn2.astype(BF16)
    row = lambda p: p[None, :]
    dir_row = lambda p: p.reshape(2, 1, -1)
    wup = _lora_weight(w_up, 0)
    aup = _lora_weight(a_up, LORA_WA // 2)
    gup = _lora_weight(g_up, 0)
    conv_w_pad = jnp.pad(conv_w, ((0, HALO - conv_w.shape[0]), (0, 0)))
    head = jnp.arange(SEG_LANES) // HEAD_DIM
    ones_bd = (head[:, None] == head[None, :]).astype(BF16)

    outs = []
    start = 0
    for x in xs:
        bsz = x.shape[0]
        m = mod[start:start + bsz]
        start += bsz
        z_conv, z_rwkv = _inproj(x, m, row(g_pre_mix), w_in_bf)
        o_fwd, o_bwd = _rwkv(z_rwkv, dir_row(mu_shift), dir_row(w0), wup, dir_row(a0), aup, gup,
                             dir_row(k_k), dir_row(k_a), dir_row(r_k), dir_row(ln_x_w),
                             dir_row(ln_x_b), ones_bd)
        outs.append(_outproj_ffn(x, z_conv, o_fwd, o_bwd, m, conv_w_pad, row(g_post_mix),
                                 row(g_pre_ffn), row(g_post_ffn), w_out_bf, w1_bf, w2_bf))
    return outs


def kernel(x_prompt, x_sample, c_prompt, c_sample, w_ada, b_ada, g_pre_mix, g_post_mix, w_in,
           conv_w, mu_shift, w0, w_up, a0, a_up, g_up, k_k, k_a, r_k, ln_x_w, ln_x_b, w_out,
           g_pre_ffn, g_post_ffn, w_ffn1, w_ffn2):
    xs = [x_prompt, x_sample]
    for l in range(w_ada.shape[0]):
        xs = _layer(xs, [c_prompt, c_sample], w_ada[l], b_ada[l], g_pre_mix[l], g_post_mix[l],
                    w_in[l], conv_w[l], mu_shift[l], w0[l], w_up[l], a0[l], a_up[l], g_up[l],
                    k_k[l], k_a[l], r_k[l], ln_x_w[l], ln_x_b[l], w_out[l], g_pre_ffn[l],
                    g_post_ffn[l], w_ffn1[l], w_ffn2[l])
    return (xs[0], xs[1])
```

```python
import math

import numpy as np

import jax
import jax.numpy as jnp
from jax import lax
from jax.experimental import pallas as pl
from jax.experimental.pallas import tpu as pltpu

D_MODEL = 1024
D_CONV = 512
D_RWKV = 512
HEAD_DIM = 64
LORA_WA = 128
GATE_RANK = 128
D_RWKV_IN = 3 * D_RWKV + LORA_WA + GATE_RANK
D_CONV_IN = 3 * D_CONV
D_CONV_MIX = 2 * D_CONV
D_FF = 4096
NORM_EPS = 1e-6
GN_EPS = 64e-5
LOG2_DECAY_SCALE = -math.exp(-0.5) * math.log2(math.e)

CHUNK = 64
PAIR = 2 * HEAD_DIM
N_PAIR = D_RWKV // PAIR
HALO = 8
SEG_LANES = 256

TILE_IN = 1024
TILE_SCAN = 256
TILE_OUT = 1024
MIN_OUT_TILES = 16
ROW_BLOCK = 256
FF_CHUNK = 1024
CARRY_AFTER = (3, 8)
VMEM_LIMIT = 56 * 1024 * 1024
VMEM_LIMIT_OUT = 60 * 1024 * 1024

F32 = jnp.float32
BF16 = jnp.bfloat16


def _dot(a, b):
    return jnp.dot(a, b, preferred_element_type=F32)


def _dot_nt(a, b):
    return lax.dot_general(a, b, (((1,), (1,)), ((), ())), preferred_element_type=F32)


def _dot_tn(a, b):
    return lax.dot_general(a, b, (((0,), (0,)), ((), ())), preferred_element_type=F32)


def _bf(x):
    return x.astype(BF16)


def _split2(x):
    hi = x.astype(BF16)
    return hi, (x - hi.astype(F32)).astype(BF16)


def _sigmoid(x):
    return 1.0 / (1.0 + jnp.exp(-x))


def _rms(x, g):
    return x * lax.rsqrt(jnp.mean(x * x, axis=-1, keepdims=True) + NORM_EPS) * g


def _seg_sum(x, ones_bd):
    xb = _bf(x)
    return jnp.concatenate([_dot(xb[:, j:j + SEG_LANES], ones_bd)
                            for j in range(0, x.shape[1], SEG_LANES)], axis=1)


def _row_blocks(rows):
    return [slice(j, j + ROW_BLOCK) for j in range(0, rows, ROW_BLOCK)]


def _mod_kernel(c_ref, w_ref, b_ref, o_ref):
    c = c_ref[...]
    s = c * _sigmoid(c)
    rows = s.shape[0]
    s_hi, s_lo = _split2(s)
    w_hi, w_lo = _split2(w_ref[...])
    hi = _dot(jnp.concatenate([s_hi, s_lo], axis=0), w_hi)
    o_ref[...] = hi[:rows] + hi[rows:] + _dot(s_hi, w_lo) + b_ref[...]


def _modulation(c_all, w_ada, b_ada):
    rows = c_all.shape[0]
    n_blk = w_ada.shape[1] // D_MODEL
    return pl.pallas_call(
        _mod_kernel,
        grid=(n_blk,),
        in_specs=[pl.BlockSpec((rows, D_MODEL), lambda j: (0, 0)),
                  pl.BlockSpec((D_MODEL, D_MODEL), lambda j: (0, j)),
                  pl.BlockSpec((1, D_MODEL), lambda j: (0, j))],
        out_specs=pl.BlockSpec((rows, D_MODEL), lambda j: (0, j)),
        out_shape=jax.ShapeDtypeStruct((rows, w_ada.shape[1]), F32),
        name="adaln_mod",
    )(c_all, w_ada, b_ada)


def _inproj_kernel(x_ref, mod_ref, g_ref, w_ref, zc_ref, zr_ref):
    m = mod_ref[0]
    for rows in _row_blocks(x_ref.shape[1]):
        h = _rms(x_ref[0, rows, :], g_ref[...]) * (1.0 + m[1:2]) + m[0:1]
        hb = h.astype(BF16)
        zc = _dot(hb, w_ref[:, :D_CONV_IN])
        zc_ref[0, rows, :D_CONV] = zc[:, :D_CONV]
        zc_ref[0, rows, D_CONV:] = zc[:, D_CONV:2 * D_CONV] * zc[:, 2 * D_CONV:]
        zr_ref[0, rows, :] = _dot(hb, w_ref[:, D_CONV_IN:])


def _inproj(x, mod, g_pre, w_in_bf):
    bsz, seq, _ = x.shape
    tm = TILE_IN
    d_in = w_in_bf.shape[1]
    return pl.pallas_call(
        _inproj_kernel,
        grid=(bsz, seq // tm),
        in_specs=[pl.BlockSpec((1, tm, D_MODEL), lambda b, i: (b, i, 0)),
                  pl.BlockSpec((1, 6, D_MODEL), lambda b, i: (b, 0, 0)),
                  pl.BlockSpec((1, D_MODEL), lambda b, i: (0, 0)),
                  pl.BlockSpec((D_MODEL, d_in), lambda b, i: (0, 0),
                               pipeline_mode=pl.Buffered(1))],
        out_specs=[pl.BlockSpec((1, tm, D_CONV_MIX), lambda b, i: (b, i, 0)),
                   pl.BlockSpec((1, tm, D_RWKV_IN), lambda b, i: (b, i, 0))],
        out_shape=[jax.ShapeDtypeStruct((bsz, seq, D_CONV_MIX), F32),
                   jax.ShapeDtypeStruct((bsz, seq, D_RWKV_IN), F32)],
        compiler_params=pltpu.CompilerParams(
            dimension_semantics=("arbitrary", "arbitrary"), vmem_limit_bytes=VMEM_LIMIT),
        name="in_proj",
    )(x, mod, g_pre, w_in_bf)


def _time_masks(direction):
    sign = 1 if direction == 0 else -1
    row = lax.broadcasted_iota(jnp.int32, (CHUNK, PAIR), 0)
    lane = lax.broadcasted_iota(jnp.int32, (CHUNK, PAIR), 1)
    delta = (row - jnp.bitwise_and(lane, HEAD_DIM - 1)) * sign
    q_row = lax.broadcasted_iota(jnp.int32, (PAIR, PAIR), 0)
    q_lane = lax.broadcasted_iota(jnp.int32, (PAIR, PAIR), 1)
    return dict(strict=delta > 0, incl=delta >= 0,
                eye=jnp.where(delta == 0, 1.0, 0.0).astype(F32),
                lane_lo=lane < HEAD_DIM,
                bd=(q_row < HEAD_DIM) == (q_lane < HEAD_DIM))


def _block_diag(y, lane_lo):
    return _bf(jnp.concatenate([jnp.where(lane_lo, y, 0.0), jnp.where(lane_lo, 0.0, y)], axis=0))


def _weave(main, *riders):
    n = 0
    for _ in main:
        n += 1
        for stages, positions in riders:
            for _ in range(positions.count(n)):
                next(stages, None)
    for stages, _ in riders:
        for _ in stages:
            pass


def _affine_stages(units, masks):
    lo = masks[0]["lane_lo"]
    bd = masks[0]["bd"]
    for u in units:
        u["aa"] = _dot_nt(_bf(jnp.concatenate([u["at"], u["rt"]], axis=0)),
                          jnp.concatenate([_block_diag(u["bt"], lo), _block_diag(u["kt"], lo)],
                                          axis=0))
    yield
    for u in units:
        m = masks[u["d"]]
        aa = u.pop("aa")
        u["n_ab"] = jnp.where(m["strict"], aa[:CHUNK, :PAIR], 0.0)
        u["a_ak"] = jnp.where(m["strict"], aa[:CHUNK, PAIR:], 0.0)
        u["a_rb"] = jnp.where(m["incl"], aa[CHUNK:, :PAIR], 0.0)
        u["a_rk"] = jnp.where(m["incl"], aa[CHUNK:, PAIR:], 0.0)
    for u in units:
        u["x"] = masks[u["d"]]["eye"] + u["n_ab"]
        u["pw"] = _dot(_bf(u["n_ab"]), _block_diag(u["n_ab"], lo))
    yield
    for u in units:
        av = _dot(_bf(jnp.concatenate([u["a_ak"], u["a_rk"]], axis=0)), _block_diag(u["v"], lo))
        u["akv"], u["rkv"] = av[:CHUNK], av[CHUNK:]
    yield
    width = 2
    while 2 * width < CHUNK:
        for u in units:
            zz = _dot(_bf(u["pw"]), jnp.concatenate([_block_diag(u["pw"], lo),
                                                     _block_diag(u["x"], lo)], axis=1))
            u["pw"] = zz[:, :PAIR]
            u["x"] = u["x"] + zz[:, PAIR:]
        yield
        width *= 2
    for u in units:
        u["x"] = u["x"] + _dot(_bf(u["pw"]), _block_diag(u["x"], lo))
    yield
    for u in units:
        wv = _dot(_bf(u["x"]), jnp.concatenate([_block_diag(u["at_true"], lo),
                                                _block_diag(u["akv"], lo)], axis=1))
        u["wt"], u["vt"] = wv[:, :PAIR], wv[:, PAIR:]
    yield
    for u in units:
        qy = _dot(_bf(u["a_rb"]), jnp.concatenate([_block_diag(u["wt"], lo),
                                                   _block_diag(u["vt"], lo)], axis=1))
        u["qhat"] = u["rt_true"] + qy[:, :PAIR]
        u["y0"] = qy[:, PAIR:] + u["rkv"]
    yield
    for u in units:
        u["mc"] = jnp.where(bd, _dot_tn(_bf(u["wt"]), _bf(u["bh"])), 0.0)
    yield
    for u in units:
        u["c0"] = jnp.where(bd, _dot_tn(_bf(jnp.concatenate([u["vt"], u["v"]], axis=0)),
                                        _bf(jnp.concatenate([u["bh"], u["kh"]], axis=0))), 0.0)
    yield


def _scan_prepare(direction, r, k, v, a, b, le, tmask):
    tt = r.shape[0]
    le_hi, le_lo = _split2(le)
    cum = _dot(tmask, le_hi) + _dot(tmask, le_lo)
    last = CHUNK - 1 if direction == 0 else 0
    tot = jnp.concatenate(
        [jnp.broadcast_to(cum[c * CHUNK + last:c * CHUNK + last + 1], (CHUNK, D_RWKV))
         for c in range(tt // CHUNK)], axis=0)
    off = 0.5 * tot
    ci = cum - off
    e_off = jnp.exp2(off)
    e_neg = jnp.exp2(-ci)
    tile = dict(at=a * jnp.exp2(ci - le), rt=r * jnp.exp2(ci), bt=b * e_neg, kt=k * e_neg, v=v)
    tile["bh"] = tile["bt"] * e_off
    tile["kh"] = tile["kt"] * e_off
    tile["at_true"] = tile["at"] * e_off
    tile["rt_true"] = tile["rt"] * e_off
    units = {}
    for c in range(tt // CHUNK):
        rs = slice(c * CHUNK, (c + 1) * CHUNK)
        for p in range(N_PAIR):
            sl = slice(p * PAIR, (p + 1) * PAIR)
            units[c, p] = dict({name: t[rs, sl] for name, t in tile.items()}, d=direction)
    return units, jnp.exp2(tot)


class _TileScan:
    def __init__(self, scans, states, post, masks):
        self.n_dir = len(scans)
        self.units, self.decay = [], []
        for d, scan in enumerate(scans):
            u, dec = _scan_prepare(d, *scan)
            self.units.append(u)
            self.decay.append(dec)
        self.n_chunk = self.decay[0].shape[0] // CHUNK
        self.states = states
        self.post = post
        self.masks = masks
        self.y_rows = [[None] * self.n_chunk for _ in scans]

    def chunk_of(self, d, j):
        return j if d == 0 else self.n_chunk - 1 - j

    def affine(self, steps):
        units = [self.units[d][self.chunk_of(d, j), p]
                 for j in steps for d in range(self.n_dir) for p in range(N_PAIR)]
        return _affine_stages(units, self.masks)

    def carry(self, j):
        for d in range(self.n_dir):
            c = self.chunk_of(d, j)
            ys = []
            for p in range(N_PAIR):
                u = self.units[d][c, p]
                s = self.states[d][p]
                s_bf = _bf(s)
                ys.append(_dot_nt(_bf(u["qhat"]), s_bf) + u["y0"])
                p_c = self.decay[d][c * CHUNK:c * CHUNK + 1, p * PAIR:(p + 1) * PAIR]
                self.states[d][p] = s * p_c + _dot(s_bf, _bf(u["mc"])) + u["c0"]
            self.y_rows[d][c] = jnp.concatenate(ys, axis=1)
        yield

    def finish(self, steps):
        stages = []
        for d in range(self.n_dir):
            cs = sorted(self.chunk_of(d, j) for j in steps)
            rows = slice(cs[0] * CHUNK, (cs[-1] + 1) * CHUNK)
            y = jnp.concatenate([self.y_rows[d][c] for c in cs], axis=0)
            stages.append(_post_stages(y, rows, *self.post[d]))
        for _ in range(3):
            for st in stages:
                next(st, None)
            yield


def _chain(*stage_iters):
    for it in stage_iters:
        yield from it


def _lockstep(stage_iters):
    live = list(stage_iters)
    while live:
        for it in list(live):
            try:
                next(it)
            except StopIteration:
                live.remove(it)
        if live:
            yield


def _pre_stages(direction, z, edge, mu, w0, wup, a0, aup, gup, k_k, k_a, ones_bd, out):
    tt = z.shape[0]
    row_id = lax.broadcasted_iota(jnp.int32, (HALO, 1), 0)
    if direction == 0:
        prev = pltpu.roll(z, 1, 0)
        prev = jnp.concatenate([jnp.where(row_id == 0, edge, prev[:HALO]), prev[HALO:]], axis=0)
    else:
        prev = pltpu.roll(z, tt - 1, 0)
        prev = jnp.concatenate([prev[:tt - HALO],
                                jnp.where(row_id == HALO - 1, edge, prev[tt - HALO:])], axis=0)
    zs = z + mu * (prev - z)
    r = zs[:, 0:D_RWKV]
    k = zs[:, D_RWKV:2 * D_RWKV]
    v = zs[:, 2 * D_RWKV:3 * D_RWKV]
    lora = zs[:, 3 * D_RWKV:3 * D_RWKV + LORA_WA]
    gl = zs[:, 3 * D_RWKV + LORA_WA:]

    w_lora = _dot(_bf(jnp.tanh(lora)), wup)
    a_lora = _dot(_bf(lora), aup)
    g = _dot(_bf(_sigmoid(gl)), gup)
    yield
    le = _sigmoid(w0 + w_lora) * LOG2_DECAY_SCALE
    a_sig = _sigmoid(a0 + a_lora)
    kk = k * k_k
    kk_sq = _seg_sum(kk * kk, ones_bd)
    yield
    kk = kk * lax.rsqrt(jnp.maximum(kk_sq, 1e-24))
    k = k * (1.0 + (a_sig - 1.0) * k_a)
    out.append((r, k, v, -kk, kk * a_sig, le, g))


def _post_stages(y, rows, r, k, v, g, r_k, ln_w, ln_b, ones_bd, o_ref, row0):
    mean = _seg_sum(y, ones_bd) * (1.0 / HEAD_DIM)
    rk_sum = _seg_sum(r[rows] * k[rows] * r_k, ones_bd)
    yield
    yc = y - mean
    var = _seg_sum(yc * yc, ones_bd) * (1.0 / HEAD_DIM)
    yield
    yn = yc * lax.rsqrt(var + GN_EPS) * ln_w + ln_b
    o_ref[0, row0 + rows.start:row0 + rows.stop, :] = (yn + rk_sum * v[rows]) * g[rows]


def _rwkv_kernel(zf_ref, hf_ref, zb_ref, hb_ref, mu_ref, w0_ref, wup_ref, a0_ref, aup_ref,
                 gup_ref, kk_ref, ka_ref, rk_ref, lnw_ref, lnb_ref, ones_ref, tmask_ref,
                 of_ref, ob_ref, s_ref):
    i = pl.program_id(1)

    @pl.when(i == 0)
    def _():
        s_ref[...] = jnp.zeros_like(s_ref)

    tt = zf_ref.shape[1] // 2
    ones_bd = ones_ref[...]
    masks = [_time_masks(0), _time_masks(1)]
    z_refs = (zf_ref, zb_ref)
    o_refs = (of_ref, ob_ref)
    states = [[s_ref[d, p] for p in range(N_PAIR)] for d in range(2)]
    row0 = ((0, tt), (tt, 0))

    def edge(sub, d):
        if sub == 0:
            halo = hf_ref[0, HALO - 1:HALO, :] if d == 0 else hb_ref[0, 0:1, :]
            return jnp.where(i > 0, halo, 0.0)
        return zf_ref[0, tt - 1:tt, :] if d == 0 else zb_ref[0, tt:tt + 1, :]

    def pre_stages(sub, out):
        return _lockstep([
            _pre_stages(d, z_refs[d][0, row0[sub][d]:row0[sub][d] + tt, :], edge(sub, d),
                        mu_ref[d], w0_ref[d], wup_ref[d], a0_ref[d], aup_ref[d], gup_ref[d],
                        kk_ref[d], ka_ref[d], ones_bd, out[d]) for d in range(2)])

    def tile_scan(sub, pre):
        pre = [p[0] for p in pre]
        post = [(pre[d][0], pre[d][1], pre[d][2], pre[d][6], rk_ref[d], lnw_ref[d], lnb_ref[d],
                 ones_bd, o_refs[d], row0[sub][d]) for d in range(2)]
        return _TileScan([pre[d][:6] + (tmask_ref[d],) for d in range(2)], states, post, masks)

    def run(stages):
        for _ in stages:
            pass

    def scan_first_half(t):
        half = t.n_chunk // 2
        run(t.affine(range(half)))
        _weave(t.affine(range(half, t.n_chunk)),
               *[(t.carry(j), (CARRY_AFTER[j],)) for j in range(half)])
        return range(half, t.n_chunk)

    pre0 = [[], []]
    run(pre_stages(0, pre0))
    t0 = tile_scan(0, pre0)
    rest = list(scan_first_half(t0))
    run(t0.carry(rest[0]))
    pre1 = [[], []]
    _weave(pre_stages(1, pre1),
           (_chain(*[t0.carry(j) for j in rest[1:]], t0.finish(range(t0.n_chunk))), (1, 2)))
    t1 = tile_scan(1, pre1)
    for j in scan_first_half(t1):
        run(t1.carry(j))
    run(t1.finish(range(t1.n_chunk)))
    for d in range(2):
        for p in range(N_PAIR):
            s_ref[d, p] = states[d][p]


def _chunk_sum_masks(tt):
    t = np.arange(tt)
    same = (t[:, None] // CHUNK) == (t[None, :] // CHUNK)
    fwd = same & (t[None, :] <= t[:, None])
    bwd = same & (t[None, :] >= t[:, None])
    return jnp.asarray(np.stack([fwd, bwd]), dtype=BF16)


def _rwkv(z_rwkv, mu, w0, wup, a0, aup, gup, k_k, k_a, r_k, ln_w, ln_b, ones_bd):
    bsz, seq, _ = z_rwkv.shape
    tt = 2 * TILE_SCAN
    n_tile = seq // tt
    blk = tt // HALO
    last_blk = n_tile * blk - 1

    def whole(arr):
        return pl.BlockSpec(arr.shape, lambda b, i: (0,) * arr.ndim)

    tmask = _chunk_sum_masks(TILE_SCAN)
    params = (mu, w0, wup, a0, aup, gup, k_k, k_a, r_k, ln_w, ln_b, ones_bd, tmask)
    tile = (1, tt, D_RWKV_IN)
    halo = (1, HALO, D_RWKV_IN)
    out_tile = (1, tt, D_RWKV)
    return pl.pallas_call(
        _rwkv_kernel,
        grid=(bsz, n_tile),
        in_specs=[pl.BlockSpec(tile, lambda b, i: (b, i, 0)),
                  pl.BlockSpec(halo, lambda b, i: (b, jnp.maximum(i * blk - 1, 0), 0)),
                  pl.BlockSpec(tile, lambda b, i: (b, n_tile - 1 - i, 0)),
                  pl.BlockSpec(halo, lambda b, i: (b, jnp.minimum((n_tile - i) * blk, last_blk), 0)),
                  ] + [whole(p) for p in params],
        out_specs=[pl.BlockSpec(out_tile, lambda b, i: (b, i, 0)),
                   pl.BlockSpec(out_tile, lambda b, i: (b, n_tile - 1 - i, 0))],
        out_shape=[jax.ShapeDtypeStruct((bsz, seq, D_RWKV), F32)] * 2,
        scratch_shapes=[pltpu.VMEM((2, N_PAIR, PAIR, PAIR), F32)],
        compiler_params=pltpu.CompilerParams(
            dimension_semantics=("arbitrary", "arbitrary"), vmem_limit_bytes=VMEM_LIMIT),
        name="rwkv7_scan",
    )(z_rwkv, z_rwkv, z_rwkv, z_rwkv, *params)


def _out_kernel(x_ref, zc_ref, zp_ref, zn_ref, of_ref, ob_ref, mod_ref, cw_ref, g_post_mix_ref,
                g_pre_ffn_ref, g_post_ffn_ref, wout_ref, w1_ref, w2_ref, y_ref):
    i = pl.program_id(1)
    n = pl.num_programs(1)
    tm = x_ref.shape[1]
    m = mod_ref[0]
    cw = cw_ref[...]
    blocks = _row_blocks(tm)
    row_id = lax.broadcasted_iota(jnp.int32, (HALO, 1), 0)

    def gated(zrows):
        return zrows[:, D_CONV:]

    def mixer_inputs(rows):
        zc = zc_ref[0, rows, :]
        cu = gated(zc)
        if rows.start == 0:
            cu_prev = jnp.where(i > 0, gated(zp_ref[0, HALO - 1:HALO, :]), 0.0)
        else:
            cu_prev = gated(zc_ref[0, rows.start - 1:rows.start, :])
        if rows.stop == tm:
            cu_next = jnp.where(i < n - 1, gated(zn_ref[0, 0:1, :]), 0.0)
        else:
            cu_next = gated(zc_ref[0, rows.stop:rows.stop + 1, :])
        before = pltpu.roll(cu, 1, 0)
        before = jnp.concatenate([jnp.where(row_id == 0, cu_prev, before[:HALO]), before[HALO:]],
                                 axis=0)
        after = pltpu.roll(cu, ROW_BLOCK - 1, 0)
        after = jnp.concatenate([after[:ROW_BLOCK - HALO],
                                 jnp.where(row_id == HALO - 1, cu_next, after[ROW_BLOCK - HALO:])],
                                axis=0)
        y_conv = zc[:, 0:D_CONV] * (cw[0:1] * before + cw[1:2] * cu + cw[2:3] * after)
        y_rwkv = of_ref[0, rows, :] + ob_ref[0, rows, :]
        return y_conv.astype(BF16), y_rwkv.astype(BF16)

    def out_proj(inputs):
        y_conv, y_rwkv = inputs
        return _dot(y_conv, wout_ref[:D_CONV, :]) + _dot(y_rwkv, wout_ref[D_CONV:, :])

    def mlp_input(rows, mix):
        x1 = x_ref[0, rows, :] + m[2:3] * _rms(mix, g_post_mix_ref[...])
        h = _rms(x1, g_pre_ffn_ref[...]) * (1.0 + m[4:5]) + m[3:4]
        return x1, h.astype(BF16)

    def mlp(hb):
        f = None
        for j in range(0, D_FF, FF_CHUNK):
            hid = jnp.square(jnp.maximum(_dot(hb, w1_ref[:, j:j + FF_CHUNK]), 0.0))
            part = _dot(hid.astype(BF16), w2_ref[j:j + FF_CHUNK, :])
            f = part if f is None else f + part
        return f

    mixes = []
    for rows in blocks:
        mixes.append(out_proj(mixer_inputs(rows)))
    mid = [mlp_input(rows, mix) for rows, mix in zip(blocks, mixes)]
    for rows, (x1, hb) in zip(blocks, mid):
        y_ref[0, rows, :] = x1 + m[5:6] * _rms(mlp(hb), g_post_ffn_ref[...])


def _outproj_ffn(x, z_conv, o_fwd, o_bwd, mod, conv_w_pad, g_post_mix, g_pre_ffn, g_post_ffn,
                 w_out_bf, w1_bf, w2_bf):
    bsz, seq, _ = x.shape
    tm = TILE_OUT if seq // TILE_OUT >= MIN_OUT_TILES else TILE_OUT // 2
    n_tile = seq // tm
    blk = tm // HALO

    def const(shape):
        return pl.BlockSpec(shape, lambda b, i: (0,) * len(shape),
                            pipeline_mode=pl.Buffered(1))

    return pl.pallas_call(
        _out_kernel,
        grid=(bsz, n_tile),
        in_specs=[pl.BlockSpec((1, tm, D_MODEL), lambda b, i: (b, i, 0)),
                  pl.BlockSpec((1, tm, D_CONV_MIX), lambda b, i: (b, i, 0)),
                  pl.BlockSpec((1, HALO, D_CONV_MIX),
                               lambda b, i: (b, jnp.maximum(i * blk - 1, 0), 0)),
                  pl.BlockSpec((1, HALO, D_CONV_MIX),
                               lambda b, i: (b, jnp.minimum((i + 1) * blk, n_tile * blk - 1), 0)),
                  pl.BlockSpec((1, tm, D_RWKV), lambda b, i: (b, i, 0)),
                  pl.BlockSpec((1, tm, D_RWKV), lambda b, i: (b, i, 0)),
                  pl.BlockSpec((1, 6, D_MODEL), lambda b, i: (b, 0, 0)),
                  const((HALO, D_CONV)), const((1, D_MODEL)), const((1, D_MODEL)),
                  const((1, D_MODEL)), const((D_MODEL, D_MODEL)), const((D_MODEL, D_FF)),
                  const((D_FF, D_MODEL))],
        out_specs=pl.BlockSpec((1, tm, D_MODEL), lambda b, i: (b, i, 0)),
        out_shape=jax.ShapeDtypeStruct((bsz, seq, D_MODEL), F32),
        compiler_params=pltpu.CompilerParams(
            dimension_semantics=("arbitrary", "arbitrary"), vmem_limit_bytes=VMEM_LIMIT_OUT),
        name="out_proj_ffn",
    )(x, z_conv, z_conv, z_conv, o_fwd, o_bwd, mod, conv_w_pad, g_post_mix, g_pre_ffn,
      g_post_ffn, w_out_bf, w1_bf, w2_bf)


def _lora_weight(w, top):
    return _bf(jnp.pad(w, ((0, 0), (top, LORA_WA - top - w.shape[1]), (0, 0))))


def _layer(xs, cs, w_ada, b_ada, g_pre_mix, g_post_mix, w_in, conv_w, mu_shift, w0, w_up, a0,
           a_up, g_up, k_k, k_a, r_k, ln_x_w, ln_x_b, w_out, g_pre_ffn, g_post_ffn, w_ffn1,
           w_ffn2):
    n_seq = sum(c.shape[0] for c in cs)
    c_all = jnp.pad(jnp.concatenate(cs, axis=0), ((0, HALO - n_seq), (0, 0)))
    mod = _modulation(c_all, w_ada, b_ada[None, :]).reshape(HALO, 6, D_MODEL)

    w_in_bf = w_in.astype(BF16)
    w_out_bf = w_out.astype(BF16)
    w1_bf = w_ffn1.astype(BF16)
    w2_bf = w_ffn2.astype(BF16)
    row = lambda p: p[None, :]
    dir_row = lambda p: p.reshape(2, 1, -1)
    wup = _lora_weight(w_up, 0)
    aup = _lora_weight(a_up, LORA_WA // 2)
    gup = _lora_weight(g_up, 0)
    conv_w_pad = jnp.pad(conv_w, ((0, HALO - conv_w.shape[0]), (0, 0)))
    head = jnp.arange(SEG_LANES) // HEAD_DIM
    ones_bd = (head[:, None] == head[None, :]).astype(BF16)

    outs = []
    start = 0
    for x in xs:
        bsz = x.shape[0]
        m = mod[start:start + bsz]
        start += bsz
        z_conv, z_rwkv = _inproj(x, m, row(g_pre_mix), w_in_bf)
        o_fwd, o_bwd = _rwkv(z_rwkv, dir_row(mu_shift), dir_row(w0), wup, dir_row(a0), aup, gup,
                             dir_row(k_k), dir_row(k_a), dir_row(r_k), dir_row(ln_x_w),
                             dir_row(ln_x_b), ones_bd)
        outs.append(_outproj_ffn(x, z_conv, o_fwd, o_bwd, m, conv_w_pad, row(g_post_mix),
                                 row(g_pre_ffn), row(g_post_ffn), w_out_bf, w1_bf, w2_bf))
    return outs


def kernel(x_prompt, x_sample, c_prompt, c_sample, w_ada, b_ada, g_pre_mix, g_post_mix, w_in,
           conv_w, mu_shift, w0, w_up, a0, a_up, g_up, k_k, k_a, r_k, ln_x_w, ln_x_b, w_out,
           g_pre_ffn, g_post_ffn, w_ffn1, w_ffn2):
    xs = [x_prompt, x_sample]
    for l in range(w_ada.shape[0]):
        xs = _layer(xs, [c_prompt, c_sample], w_ada[l], b_ada[l], g_pre_mix[l], g_post_mix[l],
                    w_in[l], conv_w[l], mu_shift[l], w0[l], w_up[l], a0[l], a_up[l], g_up[l],
                    k_k[l], k_a[l], r_k[l], ln_x_w[l], ln_x_b[l], w_out[l], g_pre_ffn[l],
                    g_post_ffn[l], w_ffn1[l], w_ffn2[l])
    return (xs[0], xs[1])
```

```python
import math

import numpy as np

import jax
import jax.numpy as jnp
from jax import lax
from jax.experimental import pallas as pl
from jax.experimental.pallas import tpu as pltpu

D_MODEL = 1024
D_CONV = 512
D_RWKV = 512
HEAD_DIM = 64
LORA_WA = 128
GATE_RANK = 128
D_RWKV_IN = 3 * D_RWKV + LORA_WA + GATE_RANK
D_CONV_IN = 3 * D_CONV
D_CONV_MIX = 2 * D_CONV
D_FF = 4096
NORM_EPS = 1e-6
GN_EPS = 64e-5
LOG2_DECAY_SCALE = -math.exp(-0.5) * math.log2(math.e)
NEG_LOG2E = -math.log2(math.e)

CHUNK = 64
PAIR = 2 * HEAD_DIM
N_PAIR = D_RWKV // PAIR
HALO = 8
SEG_LANES = 256

TILE_IN = 1024
TILE_SCAN = 256
TILE_OUT = 1024
MIN_OUT_TILES = 16
ROW_BLOCK = 256
FF_CHUNK = 1024
CARRY_AFTER = (3, 8)
VMEM_LIMIT = 56 * 1024 * 1024
VMEM_LIMIT_OUT = 60 * 1024 * 1024

F32 = jnp.float32
BF16 = jnp.bfloat16


def _dot(a, b):
    return jnp.dot(a, b, preferred_element_type=F32)


def _dot_nt(a, b):
    return lax.dot_general(a, b, (((1,), (1,)), ((), ())), preferred_element_type=F32)


def _dot_tn(a, b):
    return lax.dot_general(a, b, (((0,), (0,)), ((), ())), preferred_element_type=F32)


def _bf(x):
    return x.astype(BF16)


def _split2(x):
    hi = x.astype(BF16)
    return hi, (x - hi.astype(F32)).astype(BF16)


def _sigmoid(x):
    return 1.0 / (1.0 + jnp.exp(-x))


def _rms(x, g):
    return x * lax.rsqrt(jnp.mean(x * x, axis=-1, keepdims=True) + NORM_EPS) * g


def _seg_sum(x, ones_bd):
    xb = _bf(x)
    return jnp.concatenate([_dot(xb[:, j:j + SEG_LANES], ones_bd)
                            for j in range(0, x.shape[1], SEG_LANES)], axis=1)


def _row_blocks(rows):
    return [slice(j, j + ROW_BLOCK) for j in range(0, rows, ROW_BLOCK)]


def _mod_kernel(c_ref, w_ref, b_ref, o_ref):
    c = c_ref[...]
    s = c * _sigmoid(c)
    rows = s.shape[0]
    s_hi, s_lo = _split2(s)
    w_hi, w_lo = _split2(w_ref[...])
    hi = _dot(jnp.concatenate([s_hi, s_lo], axis=0), w_hi)
    o_ref[...] = hi[:rows] + hi[rows:] + _dot(s_hi, w_lo) + b_ref[...]


def _modulation(c_all, w_ada, b_ada):
    rows = c_all.shape[0]
    n_blk = w_ada.shape[1] // D_MODEL
    return pl.pallas_call(
        _mod_kernel,
        grid=(n_blk,),
        in_specs=[pl.BlockSpec((rows, D_MODEL), lambda j: (0, 0)),
                  pl.BlockSpec((D_MODEL, D_MODEL), lambda j: (0, j)),
                  pl.BlockSpec((1, D_MODEL), lambda j: (0, j))],
        out_specs=pl.BlockSpec((rows, D_MODEL), lambda j: (0, j)),
        out_shape=jax.ShapeDtypeStruct((rows, w_ada.shape[1]), F32),
        name="adaln_mod",
    )(c_all, w_ada, b_ada)


def _inproj_kernel(x_ref, mod_ref, g_ref, w_ref, zc_ref, zr_ref):
    m = mod_ref[0]
    for rows in _row_blocks(x_ref.shape[1]):
        h = _rms(x_ref[0, rows, :], g_ref[...]) * (1.0 + m[1:2]) + m[0:1]
        hb = h.astype(BF16)
        zc = _dot(hb, w_ref[:, :D_CONV_IN])
        zc_ref[0, rows, :D_CONV] = zc[:, :D_CONV]
        zc_ref[0, rows, D_CONV:] = zc[:, D_CONV:2 * D_CONV] * zc[:, 2 * D_CONV:]
        zr_ref[0, rows, :] = _dot(hb, w_ref[:, D_CONV_IN:])


def _inproj(x, mod, g_pre, w_in_bf):
    bsz, seq, _ = x.shape
    tm = TILE_IN
    d_in = w_in_bf.shape[1]
    return pl.pallas_call(
        _inproj_kernel,
        grid=(bsz, seq // tm),
        in_specs=[pl.BlockSpec((1, tm, D_MODEL), lambda b, i: (b, i, 0)),
                  pl.BlockSpec((1, 6, D_MODEL), lambda b, i: (b, 0, 0)),
                  pl.BlockSpec((1, D_MODEL), lambda b, i: (0, 0)),
                  pl.BlockSpec((D_MODEL, d_in), lambda b, i: (0, 0),
                               pipeline_mode=pl.Buffered(1))],
        out_specs=[pl.BlockSpec((1, tm, D_CONV_MIX), lambda b, i: (b, i, 0)),
                   pl.BlockSpec((1, tm, D_RWKV_IN), lambda b, i: (b, i, 0))],
        out_shape=[jax.ShapeDtypeStruct((bsz, seq, D_CONV_MIX), F32),
                   jax.ShapeDtypeStruct((bsz, seq, D_RWKV_IN), F32)],
        compiler_params=pltpu.CompilerParams(
            dimension_semantics=("arbitrary", "arbitrary"), vmem_limit_bytes=VMEM_LIMIT),
        name="in_proj",
    )(x, mod, g_pre, w_in_bf)


def _time_masks(direction):
    sign = 1 if direction == 0 else -1
    row = lax.broadcasted_iota(jnp.int32, (CHUNK, PAIR), 0)
    lane = lax.broadcasted_iota(jnp.int32, (CHUNK, PAIR), 1)
    delta = (row - jnp.bitwise_and(lane, HEAD_DIM - 1)) * sign
    q_row = lax.broadcasted_iota(jnp.int32, (PAIR, PAIR), 0)
    q_lane = lax.broadcasted_iota(jnp.int32, (PAIR, PAIR), 1)
    return dict(strict=delta > 0, incl=delta >= 0,
                eye=jnp.where(delta == 0, 1.0, 0.0).astype(F32),
                lane_lo=lane < HEAD_DIM,
                bd=(q_row < HEAD_DIM) == (q_lane < HEAD_DIM))


def _block_diag(y, lane_lo):
    return _bf(jnp.concatenate([jnp.where(lane_lo, y, 0.0), jnp.where(lane_lo, 0.0, y)], axis=0))


def _weave(main, *riders):
    n = 0
    for _ in main:
        n += 1
        for stages, positions in riders:
            for _ in range(positions.count(n)):
                next(stages, None)
    for stages, _ in riders:
        for _ in stages:
            pass


def _affine_stages(units, masks):
    lo = masks[0]["lane_lo"]
    bd = masks[0]["bd"]
    for u in units:
        u["aa"] = _dot_nt(_bf(jnp.concatenate([u["at"], u["rt"]], axis=0)),
                          jnp.concatenate([_block_diag(u["bt"], lo), _block_diag(u["kt"], lo)],
                                          axis=0))
    yield
    for u in units:
        m = masks[u["d"]]
        aa = u.pop("aa")
        u["n_ab"] = jnp.where(m["strict"], aa[:CHUNK, :PAIR], 0.0)
        u["a_ak"] = jnp.where(m["strict"], aa[:CHUNK, PAIR:], 0.0)
        u["a_rb"] = jnp.where(m["incl"], aa[CHUNK:, :PAIR], 0.0)
        u["a_rk"] = jnp.where(m["incl"], aa[CHUNK:, PAIR:], 0.0)
    for u in units:
        u["x"] = masks[u["d"]]["eye"] + u["n_ab"]
        u["pw"] = _dot(_bf(u["n_ab"]), _block_diag(u["n_ab"], lo))
    yield
    for u in units:
        av = _dot(_bf(jnp.concatenate([u["a_ak"], u["a_rk"]], axis=0)), _block_diag(u["v"], lo))
        u["akv"], u["rkv"] = av[:CHUNK], av[CHUNK:]
    yield
    width = 2
    while 2 * width < CHUNK:
        for u in units:
            zz = _dot(_bf(u["pw"]), jnp.concatenate([_block_diag(u["pw"], lo),
                                                     _block_diag(u["x"], lo)], axis=1))
            u["pw"] = zz[:, :PAIR]
            u["x"] = u["x"] + zz[:, PAIR:]
        yield
        width *= 2
    for u in units:
        u["x"] = u["x"] + _dot(_bf(u["pw"]), _block_diag(u["x"], lo))
    yield
    for u in units:
        wv = _dot(_bf(u["x"]), jnp.concatenate([_block_diag(u["at_true"], lo),
                                                _block_diag(u["akv"], lo)], axis=1))
        u["wt"], u["vt"] = wv[:, :PAIR], wv[:, PAIR:]
    yield
    for u in units:
        qy = _dot(_bf(u["a_rb"]), jnp.concatenate([_block_diag(u["wt"], lo),
                                                   _block_diag(u["vt"], lo)], axis=1))
        u["qhat"] = u["rt_true"] + qy[:, :PAIR]
        u["y0"] = qy[:, PAIR:] + u["rkv"]
    yield
    for u in units:
        u["mc"] = jnp.where(bd, _dot_tn(_bf(u["wt"]), _bf(u["bh"])), 0.0)
    yield
    for u in units:
        u["c0"] = jnp.where(bd, _dot_tn(_bf(jnp.concatenate([u["vt"], u["v"]], axis=0)),
                                        _bf(jnp.concatenate([u["bh"], u["kh"]], axis=0))), 0.0)
    yield


def _scan_prepare(direction, r, k, v, a, b, le, tmask):
    tt = r.shape[0]
    le_hi, le_lo = _split2(le)
    cum = _dot(tmask, le_hi) + _dot(tmask, le_lo)
    last = CHUNK - 1 if direction == 0 else 0
    tot = jnp.concatenate(
        [jnp.broadcast_to(cum[c * CHUNK + last:c * CHUNK + last + 1], (CHUNK, D_RWKV))
         for c in range(tt // CHUNK)], axis=0)
    off = 0.5 * tot
    ci = cum - off
    e_off = jnp.exp2(off)
    e_neg = jnp.exp2(-ci)
    tile = dict(at=a * jnp.exp2(ci - le), rt=r * jnp.exp2(ci), bt=b * e_neg, kt=k * e_neg, v=v)
    tile["bh"] = tile["bt"] * e_off
    tile["kh"] = tile["kt"] * e_off
    tile["at_true"] = tile["at"] * e_off
    tile["rt_true"] = tile["rt"] * e_off
    units = {}
    for c in range(tt // CHUNK):
        rs = slice(c * CHUNK, (c + 1) * CHUNK)
        for p in range(N_PAIR):
            sl = slice(p * PAIR, (p + 1) * PAIR)
            units[c, p] = dict({name: t[rs, sl] for name, t in tile.items()}, d=direction)
    return units, jnp.exp2(tot)


class _TileScan:
    def __init__(self, scans, states, post, masks):
        self.n_dir = len(scans)
        self.units, self.decay = [], []
        for d, scan in enumerate(scans):
            u, dec = _scan_prepare(d, *scan)
            self.units.append(u)
            self.decay.append(dec)
        self.n_chunk = self.decay[0].shape[0] // CHUNK
        self.states = states
        self.post = post
        self.masks = masks
        self.y_rows = [[None] * self.n_chunk for _ in scans]

    def chunk_of(self, d, j):
        return j if d == 0 else self.n_chunk - 1 - j

    def affine(self, steps):
        units = [self.units[d][self.chunk_of(d, j), p]
                 for j in steps for d in range(self.n_dir) for p in range(N_PAIR)]
        return _affine_stages(units, self.masks)

    def carry(self, j):
        for d in range(self.n_dir):
            c = self.chunk_of(d, j)
            ys = []
            for p in range(N_PAIR):
                u = self.units[d][c, p]
                s = self.states[d][p]
                s_bf = _bf(s)
                ys.append(_dot_nt(_bf(u["qhat"]), s_bf) + u["y0"])
                p_c = self.decay[d][c * CHUNK:c * CHUNK + 1, p * PAIR:(p + 1) * PAIR]
                self.states[d][p] = s * p_c + _dot(s_bf, _bf(u["mc"])) + u["c0"]
            self.y_rows[d][c] = jnp.concatenate(ys, axis=1)
        yield

    def finish(self, steps):
        stages = []
        for d in range(self.n_dir):
            cs = sorted(self.chunk_of(d, j) for j in steps)
            rows = slice(cs[0] * CHUNK, (cs[-1] + 1) * CHUNK)
            y = jnp.concatenate([self.y_rows[d][c] for c in cs], axis=0)
            stages.append(_post_stages(y, rows, *self.post[d]))
        for _ in range(3):
            for st in stages:
                next(st, None)
            yield


def _chain(*stage_iters):
    for it in stage_iters:
        yield from it


def _lockstep(stage_iters):
    live = list(stage_iters)
    while live:
        for it in list(live):
            try:
                next(it)
            except StopIteration:
                live.remove(it)
        if live:
            yield


def _pre_stages(direction, z, edge, mu, w0, wup, a0, aup, gup, k_k, k_a, ones_bd, out):
    tt = z.shape[0]
    row_id = lax.broadcasted_iota(jnp.int32, (HALO, 1), 0)
    if direction == 0:
        prev = pltpu.roll(z, 1, 0)
        prev = jnp.concatenate([jnp.where(row_id == 0, edge, prev[:HALO]), prev[HALO:]], axis=0)
    else:
        prev = pltpu.roll(z, tt - 1, 0)
        prev = jnp.concatenate([prev[:tt - HALO],
                                jnp.where(row_id == HALO - 1, edge, prev[tt - HALO:])], axis=0)
    zs = z + mu * (prev - z)
    r = zs[:, 0:D_RWKV]
    k = zs[:, D_RWKV:2 * D_RWKV]
    v = zs[:, 2 * D_RWKV:3 * D_RWKV]
    lora = zs[:, 3 * D_RWKV:3 * D_RWKV + LORA_WA]
    gl = zs[:, 3 * D_RWKV + LORA_WA:]

    w_lora = _dot(_bf(jnp.tanh(lora)), wup)
    a_lora = _dot(_bf(lora), aup)
    g = _dot(_bf(_sigmoid(gl)), gup)
    yield
    le = LOG2_DECAY_SCALE / (1.0 + jnp.exp2(w0 + w_lora))
    a_sig = 1.0 / (1.0 + jnp.exp2(a0 + a_lora))
    kk = k * k_k
    kk_sq = _seg_sum(kk * kk, ones_bd)
    yield
    kk = kk * lax.rsqrt(jnp.maximum(kk_sq, 1e-24))
    k = k * (1.0 + (a_sig - 1.0) * k_a)
    out.append((r, k, v, -kk, kk * a_sig, le, g))


def _post_stages(y, rows, r, k, v, g, r_k, ln_w, ln_b, ones_bd, o_ref, row0):
    mean = _seg_sum(y, ones_bd) * (1.0 / HEAD_DIM)
    rk_sum = _seg_sum(r[rows] * k[rows] * r_k, ones_bd)
    yield
    yc = y - mean
    var = _seg_sum(yc * yc, ones_bd) * (1.0 / HEAD_DIM)
    yield
    yn = yc * lax.rsqrt(var + GN_EPS) * ln_w + ln_b
    o_ref[0, row0 + rows.start:row0 + rows.stop, :] = (yn + rk_sum * v[rows]) * g[rows]


def _rwkv_kernel(zf_ref, hf_ref, zb_ref, hb_ref, mu_ref, w0_ref, wup_ref, a0_ref, aup_ref,
                 gup_ref, kk_ref, ka_ref, rk_ref, lnw_ref, lnb_ref, ones_ref, tmask_ref,
                 of_ref, ob_ref, s_ref):
    i = pl.program_id(1)

    @pl.when(i == 0)
    def _():
        s_ref[...] = jnp.zeros_like(s_ref)

    tt = zf_ref.shape[1] // 2
    ones_bd = ones_ref[...]
    masks = [_time_masks(0), _time_masks(1)]
    z_refs = (zf_ref, zb_ref)
    o_refs = (of_ref, ob_ref)
    states = [[s_ref[d, p] for p in range(N_PAIR)] for d in range(2)]
    row0 = ((0, tt), (tt, 0))

    def edge(sub, d):
        if sub == 0:
            halo = hf_ref[0, HALO - 1:HALO, :] if d == 0 else hb_ref[0, 0:1, :]
            return jnp.where(i > 0, halo, 0.0)
        return zf_ref[0, tt - 1:tt, :] if d == 0 else zb_ref[0, tt:tt + 1, :]

    def pre_stages(sub, out):
        return _lockstep([
            _pre_stages(d, z_refs[d][0, row0[sub][d]:row0[sub][d] + tt, :], edge(sub, d),
                        mu_ref[d], w0_ref[d], wup_ref[d], a0_ref[d], aup_ref[d], gup_ref[d],
                        kk_ref[d], ka_ref[d], ones_bd, out[d]) for d in range(2)])

    def tile_scan(sub, pre):
        pre = [p[0] for p in pre]
        post = [(pre[d][0], pre[d][1], pre[d][2], pre[d][6], rk_ref[d], lnw_ref[d], lnb_ref[d],
                 ones_bd, o_refs[d], row0[sub][d]) for d in range(2)]
        return _TileScan([pre[d][:6] + (tmask_ref[d],) for d in range(2)], states, post, masks)

    def run(stages):
        for _ in stages:
            pass

    def scan_first_half(t):
        half = t.n_chunk // 2
        run(t.affine(range(half)))
        _weave(t.affine(range(half, t.n_chunk)),
               *[(t.carry(j), (CARRY_AFTER[j],)) for j in range(half)])
        return range(half, t.n_chunk)

    pre0 = [[], []]
    run(pre_stages(0, pre0))
    t0 = tile_scan(0, pre0)
    rest = list(scan_first_half(t0))
    run(t0.carry(rest[0]))
    pre1 = [[], []]
    _weave(pre_stages(1, pre1),
           (_chain(*[t0.carry(j) for j in rest[1:]], t0.finish(range(t0.n_chunk))), (1, 2)))
    t1 = tile_scan(1, pre1)
    for j in scan_first_half(t1):
        run(t1.carry(j))
    run(t1.finish(range(t1.n_chunk)))
    for d in range(2):
        for p in range(N_PAIR):
            s_ref[d, p] = states[d][p]


def _chunk_sum_masks(tt):
    t = np.arange(tt)
    same = (t[:, None] // CHUNK) == (t[None, :] // CHUNK)
    fwd = same & (t[None, :] <= t[:, None])
    bwd = same & (t[None, :] >= t[:, None])
    return jnp.asarray(np.stack([fwd, bwd]), dtype=BF16)


def _rwkv(z_rwkv, mu, w0, wup, a0, aup, gup, k_k, k_a, r_k, ln_w, ln_b, ones_bd):
    bsz, seq, _ = z_rwkv.shape
    tt = 2 * TILE_SCAN
    n_tile = seq // tt
    blk = tt // HALO
    last_blk = n_tile * blk - 1

    def whole(arr):
        return pl.BlockSpec(arr.shape, lambda b, i: (0,) * arr.ndim)

    tmask = _chunk_sum_masks(TILE_SCAN)
    params = (mu, w0, wup, a0, aup, gup, k_k, k_a, r_k, ln_w, ln_b, ones_bd, tmask)
    tile = (1, tt, D_RWKV_IN)
    halo = (1, HALO, D_RWKV_IN)
    out_tile = (1, tt, D_RWKV)
    return pl.pallas_call(
        _rwkv_kernel,
        grid=(bsz, n_tile),
        in_specs=[pl.BlockSpec(tile, lambda b, i: (b, i, 0)),
                  pl.BlockSpec(halo, lambda b, i: (b, jnp.maximum(i * blk - 1, 0), 0)),
                  pl.BlockSpec(tile, lambda b, i: (b, n_tile - 1 - i, 0)),
                  pl.BlockSpec(halo, lambda b, i: (b, jnp.minimum((n_tile - i) * blk, last_blk), 0)),
                  ] + [whole(p) for p in params],
        out_specs=[pl.BlockSpec(out_tile, lambda b, i: (b, i, 0)),
                   pl.BlockSpec(out_tile, lambda b, i: (b, n_tile - 1 - i, 0))],
        out_shape=[jax.ShapeDtypeStruct((bsz, seq, D_RWKV), F32)] * 2,
        scratch_shapes=[pltpu.VMEM((2, N_PAIR, PAIR, PAIR), F32)],
        compiler_params=pltpu.CompilerParams(
            dimension_semantics=("arbitrary", "arbitrary"), vmem_limit_bytes=VMEM_LIMIT),
        name="rwkv7_scan",
    )(z_rwkv, z_rwkv, z_rwkv, z_rwkv, *params)


def _out_kernel(x_ref, zc_ref, zp_ref, zn_ref, of_ref, ob_ref, mod_ref, cw_ref, g_post_mix_ref,
                g_pre_ffn_ref, g_post_ffn_ref, wout_ref, w1_ref, w2_ref, y_ref):
    i = pl.program_id(1)
    n = pl.num_programs(1)
    tm = x_ref.shape[1]
    m = mod_ref[0]
    cw = cw_ref[...]
    blocks = _row_blocks(tm)
    row_id = lax.broadcasted_iota(jnp.int32, (HALO, 1), 0)

    def gated(zrows):
        return zrows[:, D_CONV:]

    def mixer_inputs(rows):
        zc = zc_ref[0, rows, :]
        cu = gated(zc)
        if rows.start == 0:
            cu_prev = jnp.where(i > 0, gated(zp_ref[0, HALO - 1:HALO, :]), 0.0)
        else:
            cu_prev = gated(zc_ref[0, rows.start - 1:rows.start, :])
        if rows.stop == tm:
            cu_next = jnp.where(i < n - 1, gated(zn_ref[0, 0:1, :]), 0.0)
        else:
            cu_next = gated(zc_ref[0, rows.stop:rows.stop + 1, :])
        before = pltpu.roll(cu, 1, 0)
        before = jnp.concatenate([jnp.where(row_id == 0, cu_prev, before[:HALO]), before[HALO:]],
                                 axis=0)
        after = pltpu.roll(cu, ROW_BLOCK - 1, 0)
        after = jnp.concatenate([after[:ROW_BLOCK - HALO],
                                 jnp.where(row_id == HALO - 1, cu_next, after[ROW_BLOCK - HALO:])],
                                axis=0)
        y_conv = zc[:, 0:D_CONV] * (cw[0:1] * before + cw[1:2] * cu + cw[2:3] * after)
        y_rwkv = of_ref[0, rows, :] + ob_ref[0, rows, :]
        return y_conv.astype(BF16), y_rwkv.astype(BF16)

    def out_proj(inputs):
        return _dot(jnp.concatenate(inputs, axis=1), wout_ref[...])

    def mlp_input(rows, mix):
        x1 = x_ref[0, rows, :] + m[2:3] * _rms(mix, g_post_mix_ref[...])
        h = _rms(x1, g_pre_ffn_ref[...]) * (1.0 + m[4:5]) + m[3:4]
        return x1, h.astype(BF16)

    def mlp(hb):
        f = None
        for j in range(0, D_FF, FF_CHUNK):
            hid = jnp.square(jnp.maximum(_dot(hb, w1_ref[:, j:j + FF_CHUNK]), 0.0))
            part = _dot(hid.astype(BF16), w2_ref[j:j + FF_CHUNK, :])
            f = part if f is None else f + part
        return f

    mixes = []
    for rows in blocks:
        mixes.append(out_proj(mixer_inputs(rows)))
    mid = [mlp_input(rows, mix) for rows, mix in zip(blocks, mixes)]
    for rows, (x1, hb) in zip(blocks, mid):
        y_ref[0, rows, :] = x1 + m[5:6] * _rms(mlp(hb), g_post_ffn_ref[...])


def _outproj_ffn(x, z_conv, o_fwd, o_bwd, mod, conv_w_pad, g_post_mix, g_pre_ffn, g_post_ffn,
                 w_out_bf, w1_bf, w2_bf):
    bsz, seq, _ = x.shape
    tm = TILE_OUT if seq // TILE_OUT >= MIN_OUT_TILES else TILE_OUT // 2
    n_tile = seq // tm
    blk = tm // HALO

    def const(shape):
        return pl.BlockSpec(shape, lambda b, i: (0,) * len(shape),
                            pipeline_mode=pl.Buffered(1))

    return pl.pallas_call(
        _out_kernel,
        grid=(bsz, n_tile),
        in_specs=[pl.BlockSpec((1, tm, D_MODEL), lambda b, i: (b, i, 0)),
                  pl.BlockSpec((1, tm, D_CONV_MIX), lambda b, i: (b, i, 0)),
                  pl.BlockSpec((1, HALO, D_CONV_MIX),
                               lambda b, i: (b, jnp.maximum(i * blk - 1, 0), 0)),
                  pl.BlockSpec((1, HALO, D_CONV_MIX),
                               lambda b, i: (b, jnp.minimum((i + 1) * blk, n_tile * blk - 1), 0)),
                  pl.BlockSpec((1, tm, D_RWKV), lambda b, i: (b, i, 0)),
                  pl.BlockSpec((1, tm, D_RWKV), lambda b, i: (b, i, 0)),
                  pl.BlockSpec((1, 6, D_MODEL), lambda b, i: (b, 0, 0)),
                  const((HALO, D_CONV)), const((1, D_MODEL)), const((1, D_MODEL)),
                  const((1, D_MODEL)), const((D_MODEL, D_MODEL)), const((D_MODEL, D_FF)),
                  const((D_FF, D_MODEL))],
        out_specs=pl.BlockSpec((1, tm, D_MODEL), lambda b, i: (b, i, 0)),
        out_shape=jax.ShapeDtypeStruct((bsz, seq, D_MODEL), F32),
        compiler_params=pltpu.CompilerParams(
            dimension_semantics=("arbitrary", "arbitrary"), vmem_limit_bytes=VMEM_LIMIT_OUT),
        name="out_proj_ffn",
    )(x, z_conv, z_conv, z_conv, o_fwd, o_bwd, mod, conv_w_pad, g_post_mix, g_pre_ffn,
      g_post_ffn, w_out_bf, w1_bf, w2_bf)


def _lora_weight(w, top):
    return _bf(jnp.pad(w, ((0, 0), (top, LORA_WA - top - w.shape[1]), (0, 0))))


def _layer(xs, cs, w_ada, b_ada, g_pre_mix, g_post_mix, w_in, conv_w, mu_shift, w0, w_up, a0,
           a_up, g_up, k_k, k_a, r_k, ln_x_w, ln_x_b, w_out, g_pre_ffn, g_post_ffn, w_ffn1,
           w_ffn2):
    n_seq = sum(c.shape[0] for c in cs)
    c_all = jnp.pad(jnp.concatenate(cs, axis=0), ((0, HALO - n_seq), (0, 0)))
    mod = _modulation(c_all, w_ada, b_ada[None, :]).reshape(HALO, 6, D_MODEL)

    w_in_bf = w_in.astype(BF16)
    w_out_bf = w_out.astype(BF16)
    w1_bf = w_ffn1.astype(BF16)
    w2_bf = w_ffn2.astype(BF16)
    row = lambda p: p[None, :]
    dir_row = lambda p: p.reshape(2, 1, -1)
    w0, a0 = w0 * NEG_LOG2E, a0 * NEG_LOG2E
    wup = _lora_weight(w_up * NEG_LOG2E, 0)
    aup = _lora_weight(a_up * NEG_LOG2E, LORA_WA // 2)
    gup = _lora_weight(g_up, 0)
    conv_w_pad = jnp.pad(conv_w, ((0, HALO - conv_w.shape[0]), (0, 0)))
    head = jnp.arange(SEG_LANES) // HEAD_DIM
    ones_bd = (head[:, None] == head[None, :]).astype(BF16)

    outs = []
    start = 0
    for x in xs:
        bsz = x.shape[0]
        m = mod[start:start + bsz]
        start += bsz
        z_conv, z_rwkv = _inproj(x, m, row(g_pre_mix), w_in_bf)
        o_fwd, o_bwd = _rwkv(z_rwkv, dir_row(mu_shift), dir_row(w0), wup, dir_row(a0), aup, gup,
                             dir_row(k_k), dir_row(k_a), dir_row(r_k), dir_row(ln_x_w),
                             dir_row(ln_x_b), ones_bd)
        outs.append(_outproj_ffn(x, z_conv, o_fwd, o_bwd, m, conv_w_pad, row(g_post_mix),
                                 row(g_pre_ffn), row(g_post_ffn), w_out_bf, w1_bf, w2_bf))
    return outs


def kernel(x_prompt, x_sample, c_prompt, c_sample, w_ada, b_ada, g_pre_mix, g_post_mix, w_in,
           conv_w, mu_shift, w0, w_up, a0, a_up, g_up, k_k, k_a, r_k, ln_x_w, ln_x_b, w_out,
           g_pre_ffn, g_post_ffn, w_ffn1, w_ffn2):
    xs = [x_prompt, x_sample]
    for l in range(w_ada.shape[0]):
        xs = _layer(xs, [c_prompt, c_sample], w_ada[l], b_ada[l], g_pre_mix[l], g_post_mix[l],
                    w_in[l], conv_w[l], mu_shift[l], w0[l], w_up[l], a0[l], a_up[l], g_up[l],
                    k_k[l], k_a[l], r_k[l], ln_x_w[l], ln_x_b[l], w_out[l], g_pre_ffn[l],
                    g_post_ffn[l], w_ffn1[l], w_ffn2[l])
    return (xs[0], xs[1])
```
